```python
import math
import jax
import jax.numpy as jnp
from jax import lax
import numpy as np

D_MODEL = 1024
BATCH = 32
SEQ = 256
DEPTH = 2
DEC_BATCH = 4
DEC_SEQ = 1024
PAST_LEN = 512

F32 = jnp.float32
GRID_W = 64
EPS = 1e-6
GN_EPS = 1e-5
S5_WIDTH = D_MODEL // 2
S5_GROUP_CH = 16
S5_GROUPS = S5_WIDTH // S5_GROUP_CH
S5_STATE = 64
RET_WIDTH = D_MODEL // 2
RET_HEADS = 4
RET_DK = RET_WIDTH // RET_HEADS
RET_DV = RET_WIDTH // RET_HEADS
RET_CHUNK = 128
ROPE_BASE = 10000.0
HY_WIDTH = D_MODEL // 2
HY_ORDER = 2
HY_BANDS = 16
HY_EMB = 1 + 2 * HY_BANDS
HY_HIDDEN = 64
HY_DECAY_MIN = -math.log(1e-2) / 1.5
HY_DECAY_MAX = -math.log(1e-2) / 0.3
N_BRANCH = 3
IN_COLS = S5_WIDTH + 4 * RET_WIDTH + 3 * HY_WIDTH + N_BRANCH * D_MODEL
D_FF = 256 * (-(-8 * D_MODEL // (3 * 256)))

kernel_name = 'hybrid_s5_retnet_hyena_diffusion_step'


def _rms_norm(x, g):
    xf = x.astype(F32)
    y = xf * lax.rsqrt(jnp.mean(xf * xf, axis=-1, keepdims=True) + EPS)
    return (y * g.astype(F32)).astype(x.dtype)


def _head_norm(x):
    xc = x - jnp.mean(x, axis=-1, keepdims=True)
    return xc * lax.rsqrt(jnp.mean(xc * xc, axis=-1, keepdims=True) + GN_EPS)


def _diag_scan(lam_bar, bu, h0):
    if h0 is not None:
        bu = bu.at[:, 0].add(lam_bar * h0)
    a = jnp.broadcast_to(lam_bar, bu.shape)

    def combine(left, right):
        a_l, b_l = left
        a_r, b_r = right
        return a_l * a_r, a_r * b_l + b_r

    _, h = lax.associative_scan(combine, (a, bu), axis=1)
    return h


def _s5(u, lp, h0):
    bsz, L, _ = u.shape
    uf = u.astype(F32).reshape(bsz, L, S5_GROUPS, S5_GROUP_CH)
    lam = lax.complex(lp['s5_lam_re'].astype(F32), lp['s5_lam_im'].astype(F32))
    dt = jnp.exp(lp['s5_log_dt'].astype(F32))[..., None]
    lam_bar = jnp.exp(lam * dt)
    b = lax.complex(lp['s5_b_re'].astype(F32), lp['s5_b_im'].astype(F32))
    b_bar = ((lam_bar - 1.0) / lam)[..., None] * b
    c = lax.complex(lp['s5_c_re'].astype(F32), lp['s5_c_im'].astype(F32))
    bu = jnp.einsum('blgc,rgpc->rblgp', uf, b_bar)
    h_f = _diag_scan(lam_bar[0], bu[0], None if h0 is None else h0[:, 0])
    h_b = jnp.flip(_diag_scan(lam_bar[1], jnp.flip(bu[1], axis=1), None if h0 is None else h0[:, 1]), axis=1)
    y = jnp.real(jnp.einsum('blgp,gcp->blgc', h_f, c[0]) + jnp.einsum('blgp,gcp->blgc', h_b, c[1]))
    y = y.reshape(bsz, L, S5_WIDTH) + lp['s5_d'].astype(F32) * u.astype(F32)
    final = jnp.stack([h_f[:, -1], h_b[:, 0]], axis=1)
    return y.astype(u.dtype), final


def _rope_2d(x):
    L, dk = x.shape[1], x.shape[-1]
    n_rows = L // GRID_W
    rows = jnp.repeat(jnp.arange(n_rows, dtype=F32), GRID_W)
    cols = jnp.tile(jnp.arange(GRID_W, dtype=F32), n_rows)
    half = dk // 2
    n_freq = half // 2
    inv = ROPE_BASE ** (-jnp.arange(n_freq, dtype=F32) / n_freq)
    ang = jnp.concatenate([rows[:, None] * inv, cols[:, None] * inv], axis=-1)
    cos = jnp.cos(ang)[None, :, None, :]
    sin = jnp.sin(ang)[None, :, None, :]
    x1, x2 = x[..., :half], x[..., half:]
    return jnp.concatenate([x1 * cos - x2 * sin, x1 * sin + x2 * cos], axis=-1)


def _retention_dir(q, k, v, log_g, inclusive, s0=None, q0=None):
    bsz, L, H, dk = q.shape
    dv = v.shape[-1]
    n = L // RET_CHUNK
    qc = q.reshape(bsz, n, RET_CHUNK, H, dk)
    kc = k.reshape(bsz, n, RET_CHUNK, H, dk)
    vc = v.reshape(bsz, n, RET_CHUNK, H, dv)
    pos = jnp.arange(RET_CHUNK, dtype=F32)
    diff = pos[:, None] - pos[None, :]
    mask = (diff >= 0) if inclusive else (diff > 0)
    decay = jnp.where(mask[None], jnp.exp(log_g[:, None, None] * jnp.maximum(diff, 0.0)[None]), 0.0)
    scores = jnp.einsum('bnihd,bnjhd->bnhij', qc, kc) * decay
    intra = jnp.einsum('bnhij,bnjhe->bnihe', scores, vc)
    k_decay = jnp.exp(log_g[:, None] * (RET_CHUNK - 1.0 - pos)[None])
    kv = jnp.einsum('bnjhd,hj,bnjhe->nbhde', kc, k_decay, vc)
    chunk_decay = jnp.exp(log_g * RET_CHUNK)[:, None, None]

    def step(s, kv_n):
        return chunk_decay * s + kv_n, s

    s_final, s_prev = lax.scan(step, jnp.zeros((bsz, H, dk, dv), F32), kv)
    q_decay = jnp.exp(log_g[:, None] * (pos + 1.0)[None])
    cross = jnp.einsum('bnihd,nbhde,hi->bnihe', qc, s_prev, q_decay)
    out = (intra + cross).reshape(bsz, L, H, dv)
    if s0 is not None:
        t = jnp.arange(L, dtype=F32) + (1.0 if inclusive else 0.0)
        out = out + jnp.einsum('blhd,bhde,hl->blhe', q0, s0, jnp.exp(log_g[:, None] * t[None]))
    return out, s_final


def _flip_seq(a):
    return jnp.flip(a, axis=1)


def _retention_bidir(q, k, v, log_g, q_ctx=None, s0=None):
    fwd, s_f = _retention_dir(q, k, v, log_g[0], True,
                              None if s0 is None else s0[:, 0], q_ctx)
    bwd, s_b = _retention_dir(_flip_seq(q), _flip_seq(k), _flip_seq(v), log_g[1], False,
                              None if s0 is None else s0[:, 1],
                              None if q_ctx is None else _flip_seq(q_ctx))
    return fwd + _flip_seq(bwd), jnp.stack([s_f, s_b], axis=1)


def _short_conv(x, w, b):
    L = x.shape[1]
    xp = jnp.pad(x, ((0, 0), (1, 1), (0, 0)))
    return xp[:, :L] * w[0] + xp[:, 1:L + 1] * w[1] + xp[:, 2:] * w[2] + b


def _hyena_filter_spectra(L, lp):
    t = jnp.arange(L, dtype=F32)
    t_norm = t / L
    bands = jnp.linspace(1e-4, HY_BANDS - 1, HY_BANDS, dtype=F32)
    ang = (2.0 * math.pi / L) * t[:, None] * bands[None, :]
    z = jnp.concatenate([t_norm[:, None], jnp.cos(ang), -jnp.sin(ang)], axis=-1)
    freq = lp['hy_freq'].astype(F32)
    hid = jnp.sin(freq[0] * (z @ lp['hy_w1'].astype(F32) + lp['hy_b1'].astype(F32)))
    hid = jnp.sin(freq[1] * (hid @ lp['hy_w2'].astype(F32) + lp['hy_b2'].astype(F32)))
    filt = (hid @ lp['hy_w3'].astype(F32)).reshape(L, 2, HY_ORDER, HY_WIDTH)
    rate = jnp.linspace(HY_DECAY_MIN, HY_DECAY_MAX, HY_WIDTH, dtype=F32)
    filt = filt * jnp.exp(-t_norm[:, None, None, None] * rate)
    fwd, bwd = filt[:, 0], filt[:, 1]
    kern = jnp.concatenate([fwd, jnp.zeros((1, HY_ORDER, HY_WIDTH), F32), bwd[:0:-1]], axis=0)
    kern = kern * lax.rsqrt(jnp.sum(kern * kern, axis=0, keepdims=True) + EPS)
    return jnp.fft.rfft(kern, axis=0)


def _hyena(hy, lp):
    L = hy.shape[1]
    z = _short_conv(hy.astype(F32), lp['hy_conv_w'].astype(F32), lp['hy_conv_b'].astype(F32))
    x1, x2, v = jnp.split(z, 3, axis=-1)
    spec = _hyena_filter_spectra(L, lp)
    bias = lp['hy_bias'].astype(F32)
    out = v
    for o, gate in enumerate((x1, x2)):
        conv = jnp.fft.irfft(jnp.fft.rfft(out, n=2 * L, axis=1) * spec[None, :, o], n=2 * L, axis=1)[:, :L]
        out = gate * (conv + bias[o] * out)
    return out.astype(hy.dtype)


def _mixer(h, lp, latent, s5_h0, ret_s0):
    bsz, L, _ = h.shape
    widths = [S5_WIDTH, RET_WIDTH, RET_WIDTH, RET_WIDTH, RET_WIDTH, 3 * HY_WIDTH]
    cuts = [int(cv) for cv in np.cumsum(widths)]
    u, q, k, v, g, hy, gate_logits = jnp.split(h @ lp['w_in'], cuts, axis=-1)
    y_s5, s5_state = _s5(u, lp, s5_h0)
    a, b = jnp.split(jax.nn.gelu(y_s5) @ lp['w_s5_glu'], 2, axis=-1)
    br_s5 = a * jax.nn.sigmoid(b)
    q = q.astype(F32).reshape(bsz, L, RET_HEADS, RET_DK)
    k = k.astype(F32).reshape(bsz, L, RET_HEADS, RET_DK) * (RET_DK ** -0.5)
    v = v.astype(F32).reshape(bsz, L, RET_HEADS, RET_DV)
    log_g = jnp.log1p(-jnp.exp(lp['ret_decay'].astype(F32)))
    if latent:
        ret, ret_state = _retention_bidir(_rope_2d(q), _rope_2d(k), v, log_g, q, ret_s0)
    else:
        ret, ret_state = _retention_bidir(q, k, v, log_g)
    ret = _head_norm(ret).reshape(bsz, L, RET_WIDTH) * jax.nn.silu(g.astype(F32))
    br_ret = ret.astype(h.dtype) @ lp['w_ret_o']
    br_hy = _hyena(hy, lp) @ lp['w_hy_o']
    gates = jax.nn.sigmoid(gate_logits.astype(F32)).astype(h.dtype).reshape(bsz, L, N_BRANCH, D_MODEL)
    merged = gates[:, :, 0] * br_s5 + gates[:, :, 1] * br_ret + gates[:, :, 2] * br_hy
    return merged @ lp['w_out'], s5_state, ret_state


def _swiglu(h, w_in, w_out):
    a, b = jnp.split(h @ w_in, 2, axis=-1)
    return (jax.nn.silu(a) * b) @ w_out


def _layer(x, cond, lp, latent, s5_h0, ret_s0):
    mod = (cond @ lp['w_mod'] + lp['b_mod'])[:, None, :]
    sh1, sc1, g1, sh2, sc2, g2 = jnp.split(mod, 6, axis=-1)
    h = _rms_norm(x, lp['norm1']) * (1.0 + sc1) + sh1
    mix, s5_state, ret_state = _mixer(h, lp, latent, s5_h0, ret_s0)
    x = x + g1 * mix
    h = _rms_norm(x, lp['norm2']) * (1.0 + sc2) + sh2
    x = x + g2 * _swiglu(h, lp['w_ffn_in'], lp['w_ffn_out'])
    return x, s5_state, ret_state


def setup_inputs(seed: int = 0) -> dict:
    key = jax.random.key(seed)
    ks = jax.random.split(key, 36)

    def nrm(i, shape, scale):
        return scale * jax.random.normal(ks[i], shape, F32)

    s5_shape = (DEPTH, 2, S5_GROUPS, S5_STATE)
    lam_im = jnp.pi * jnp.arange(S5_STATE, dtype=F32) + nrm(12, s5_shape, 0.01)
    ret_decay = -(5.0 + jnp.arange(RET_HEADS, dtype=F32)) * math.log(2.0) + nrm(20, (DEPTH, 2, RET_HEADS), 0.05)
    return {
        'x_prompt': nrm(0, (BATCH, SEQ, D_MODEL), 1.0),
        'x_sample': nrm(1, (DEC_BATCH, DEC_SEQ, D_MODEL), 1.0),
        'state_s5': nrm(2, (DEC_BATCH, DEPTH, 2, S5_GROUPS, S5_STATE, 2), 0.1),
        'state_ret': nrm(3, (DEC_BATCH, DEPTH, 2, RET_HEADS, RET_DK, RET_DV), 0.5),
        'c': nrm(4, (DEC_BATCH, D_MODEL), 1.0),
        'c_ctx': nrm(5, (D_MODEL,), 1.0),
        'w_mod': nrm(6, (DEPTH, D_MODEL, 6 * D_MODEL), 0.5 * D_MODEL ** -0.5),
        'b_mod': nrm(7, (DEPTH, 6 * D_MODEL), 0.02),
        'norm1': 1.0 + nrm(8, (DEPTH, D_MODEL), 0.02),
        'norm2': 1.0 + nrm(9, (DEPTH, D_MODEL), 0.02),
        'w_in': nrm(10, (DEPTH, D_MODEL, IN_COLS), D_MODEL ** -0.5),
        's5_lam_re': -0.5 + nrm(11, s5_shape, 0.01),
        's5_lam_im': lam_im,
        's5_log_dt': jax.random.uniform(ks[13], (DEPTH, 2, S5_GROUPS), F32, math.log(1e-3), math.log(1e-1)),
        's5_b_re': nrm(14, (DEPTH, 2, S5_GROUPS, S5_STATE, S5_GROUP_CH), (2.0 * S5_GROUP_CH) ** -0.5),
        's5_b_im': nrm(15, (DEPTH, 2, S5_GROUPS, S5_STATE, S5_GROUP_CH), (2.0 * S5_GROUP_CH) ** -0.5),
        's5_c_re': nrm(16, (DEPTH, 2, S5_GROUPS, S5_GROUP_CH, S5_STATE), S5_STATE ** -0.5),
        's5_c_im': nrm(17, (DEPTH, 2, S5_GROUPS, S5_GROUP_CH, S5_STATE), S5_STATE ** -0.5),
        's5_d': nrm(18, (DEPTH, S5_WIDTH), 1.0),
        'w_s5_glu': nrm(19, (DEPTH, S5_WIDTH, 2 * D_MODEL), S5_WIDTH ** -0.5),
        'ret_decay': ret_decay,
        'w_ret_o': nrm(21, (DEPTH, RET_WIDTH, D_MODEL), RET_WIDTH ** -0.5),
        'hy_conv_w': nrm(22, (DEPTH, 3, 3 * HY_WIDTH), 3.0 ** -0.5),
        'hy_conv_b': nrm(23, (DEPTH, 3 * HY_WIDTH), 0.02),
        'hy_w1': nrm(24, (DEPTH, HY_EMB, HY_HIDDEN), HY_EMB ** -0.5),
        'hy_b1': nrm(25, (DEPTH, HY_HIDDEN), 0.1),
        'hy_w2': nrm(26, (DEPTH, HY_HIDDEN, HY_HIDDEN), HY_HIDDEN ** -0.5),
        'hy_b2': nrm(27, (DEPTH, HY_HIDDEN), 0.1),
        'hy_freq': 1.0 + nrm(28, (DEPTH, 2, HY_HIDDEN), 0.02),
        'hy_w3': nrm(29, (DEPTH, HY_HIDDEN, 2 * HY_ORDER * HY_WIDTH), HY_HIDDEN ** -0.5),
        'hy_bias': nrm(30, (DEPTH, HY_ORDER, HY_WIDTH), 1.0),
        'w_hy_o': nrm(31, (DEPTH, HY_WIDTH, D_MODEL), HY_WIDTH ** -0.5),
        'w_out': nrm(32, (DEPTH, D_MODEL, D_MODEL), D_MODEL ** -0.5),
        'w_ffn_in': nrm(33, (DEPTH, D_MODEL, 2 * D_FF), D_MODEL ** -0.5),
        'w_ffn_out': nrm(34, (DEPTH, D_FF, D_MODEL), D_FF ** -0.5),
        'norm_f': 1.0 + nrm(35, (D_MODEL,), 0.02),
    }


def reference(x_prompt, x_sample, state_s5, state_ret, c, c_ctx, w_mod, b_mod, norm1, norm2, w_in,
              s5_lam_re, s5_lam_im, s5_log_dt, s5_b_re, s5_b_im, s5_c_re, s5_c_im, s5_d, w_s5_glu,
              ret_decay, w_ret_o, hy_conv_w, hy_conv_b, hy_w1, hy_b1, hy_w2, hy_b2, hy_freq, hy_w3, hy_bias,
              w_hy_o, w_out, w_ffn_in, w_ffn_out, norm_f):
    cond_ctx = jax.nn.silu(c_ctx)[None, :]
    cond_lat = jax.nn.silu(c)
    s5_cache = lax.complex(state_s5[..., 0].astype(F32), state_s5[..., 1].astype(F32))
    ret_cache = state_ret.astype(F32)
    xp, xs = x_prompt, x_sample
    s5_states, ret_states = [], []
    for l in range(DEPTH):
        lp = {
            'w_mod': w_mod[l], 'b_mod': b_mod[l], 'norm1': norm1[l], 'norm2': norm2[l], 'w_in': w_in[l],
            's5_lam_re': s5_lam_re[l], 's5_lam_im': s5_lam_im[l], 's5_log_dt': s5_log_dt[l],
            's5_b_re': s5_b_re[l], 's5_b_im': s5_b_im[l], 's5_c_re': s5_c_re[l], 's5_c_im': s5_c_im[l],
            's5_d': s5_d[l], 'w_s5_glu': w_s5_glu[l], 'ret_decay': ret_decay[l], 'w_ret_o': w_ret_o[l],
            'hy_conv_w': hy_conv_w[l], 'hy_conv_b': hy_conv_b[l], 'hy_w1': hy_w1[l], 'hy_b1': hy_b1[l],
            'hy_w2': hy_w2[l], 'hy_b2': hy_b2[l], 'hy_freq': hy_freq[l], 'hy_w3': hy_w3[l],
            'hy_bias': hy_bias[l], 'w_hy_o': w_hy_o[l], 'w_out': w_out[l],
            'w_ffn_in': w_ffn_in[l], 'w_ffn_out': w_ffn_out[l],
        }
        xp, s5_st, ret_st = _layer(xp, cond_ctx, lp, False, None, None)
        s5_states.append(s5_st)
        ret_states.append(ret_st)
        xs, _, _ = _layer(xs, cond_lat, lp, True, s5_cache[:, l], ret_cache[:, l])
    y_prompt = _rms_norm(xp, norm_f)
    y_sample = _rms_norm(xs, norm_f)
    s5_new = jnp.stack(s5_states, axis=1)
    new_state_s5 = jnp.stack([jnp.real(s5_new), jnp.imag(s5_new)], axis=-1).astype(x_prompt.dtype)
    new_state_ret = jnp.stack(ret_states, axis=1).astype(x_prompt.dtype)
    return (y_prompt, y_sample, new_state_s5, new_state_ret)
```

```python
import functools
import math

import jax
import jax.numpy as jnp
import numpy as np
from jax import lax
from jax.experimental import pallas as pl
from jax.experimental.pallas import tpu as pltpu

F32 = jnp.float32
BF16 = jnp.bfloat16

D_MODEL = 1024
DEPTH = 2
GRID_W = 64
EPS = 1e-6
GN_EPS = 1e-5
S5_WIDTH = 512
S5_GROUP_CH = 16
S5_GROUPS = 32
S5_STATE = 64
RET_WIDTH = 512
RET_HEADS = 4
RET_DK = 128
ROPE_BASE = 10000.0
HY_WIDTH = 512
HY_ORDER = 2
HY_BANDS = 16
HY_DECAY_MIN = -math.log(1e-2) / 1.5
HY_DECAY_MAX = -math.log(1e-2) / 0.3
D_FF = 2816

LANES = 128
SUBLANES = 8
TOKEN_TILE = 256
S5_BLOCK_CH = 128
S5_BLOCK_GROUPS = S5_BLOCK_CH // S5_GROUP_CH
VMEM_LIMIT = 56 * 1024 * 1024


def _cparams(n_axes):
    return pltpu.CompilerParams(dimension_semantics=("arbitrary",) * n_axes, vmem_limit_bytes=VMEM_LIMIT)


def _const_spec(shape):
    nd = len(shape)
    return pl.BlockSpec(shape, lambda *_: (0,) * nd, pipeline_mode=pl.Buffered(1))


def _dot(a, b):
    return jnp.dot(a.astype(BF16), b.astype(BF16), preferred_element_type=F32)


def _rms_mod(x, g, scale, shift):
    y = x * lax.rsqrt(jnp.mean(x * x, axis=-1, keepdims=True) + EPS)
    return (y * g) * (1.0 + scale) + shift


def _mod_kernel(c_ref, w_ref, b_ref, o_ref):
    c = c_ref[...]
    cond = c * jax.nn.sigmoid(c)
    o_ref[0] = _dot(cond, w_ref[0]) + b_ref[0]


def _modulation(c_rows, w_mod, b_mod):
    n_chunk = 6
    return pl.pallas_call(
        _mod_kernel,
        grid=(DEPTH, n_chunk),
        in_specs=[
            pl.BlockSpec((SUBLANES, D_MODEL), lambda l, j: (0, 0)),
            pl.BlockSpec((1, D_MODEL, D_MODEL), lambda l, j: (l, 0, j)),
            pl.BlockSpec((1, 1, D_MODEL), lambda l, j: (l, 0, j)),
        ],
        out_specs=pl.BlockSpec((1, SUBLANES, D_MODEL), lambda l, j: (l, 0, j)),
        out_shape=jax.ShapeDtypeStruct((DEPTH, SUBLANES, 6 * D_MODEL), F32),
        compiler_params=_cparams(2),
        name="modulation",
    )(c_rows, w_mod, b_mod.reshape(DEPTH, 1, 6 * D_MODEL))


class _Path:
    def __init__(self, batch, seq, latent):
        self.batch = batch
        self.seq = seq
        self.latent = latent
        self.tiles_per_seq = seq // TOKEN_TILE
        self.n_tiles = batch * self.tiles_per_seq
        if latent:
            assert batch * 2 == SUBLANES
            self.s5_blocks = 1
            self.s5_cols = S5_BLOCK_GROUPS * S5_STATE // 2
            self.tm_width = 2 * S5_WIDTH
        else:
            assert batch % SUBLANES == 0
            self.s5_blocks = batch // SUBLANES
            self.s5_cols = S5_BLOCK_GROUPS * S5_STATE
            self.tm_width = S5_WIDTH
        self.s5_rows = seq * SUBLANES

    def mod_index(self, i):
        return 1 + i // self.tiles_per_seq if self.latent else 0

    def tm_index(self, i):
        if self.latent:
            return (i % self.tiles_per_seq, i // self.tiles_per_seq)
        return (i // SUBLANES, i % SUBLANES)


def _half_mask():
    ch = lax.broadcasted_iota(jnp.int32, (1, S5_WIDTH), 1)
    return (ch % S5_BLOCK_CH) < (S5_BLOCK_CH // 2)


_IN_SEGS = ((0, 512), (512, 2560), (2560, 4096), (4096, 7168))


def _in_proj_kernel(latent, x_ref, mod_ref, g_ref, w_ref, u_ref, qkvg_ref, hy_ref, gl_ref):
    h = _rms_mod(x_ref[...], g_ref[...], mod_ref[0, 1:2, :], mod_ref[0, 0:1, :]).astype(BF16)
    u = jnp.dot(h, w_ref[:, 0:512], preferred_element_type=F32)
    if latent:
        m0 = _half_mask()
        u_ref[:, 0:S5_WIDTH] = jnp.where(m0, u, 0.0)
        u_ref[:, S5_WIDTH:] = jnp.where(m0, 0.0, u)
    else:
        u_ref[...] = u
    for ref, (lo, hi) in zip((qkvg_ref, hy_ref, gl_ref), _IN_SEGS[1:]):
        ref[...] = jnp.dot(h, w_ref[:, lo:hi], preferred_element_type=F32)


def _in_proj(path, x, mod, norm_g, w_in):
    rows = x.shape[0]
    tok = lambda i: (i, 0)
    outs = pl.pallas_call(
        functools.partial(_in_proj_kernel, path.latent),
        grid=(path.n_tiles,),
        in_specs=[
            pl.BlockSpec((TOKEN_TILE, D_MODEL), tok),
            pl.BlockSpec((1, 6, D_MODEL), lambda i: (path.mod_index(i), 0, 0)),
            _const_spec((1, D_MODEL)),
            _const_spec(w_in.shape),
        ],
        out_specs=[
            pl.BlockSpec((TOKEN_TILE, path.tm_width), path.tm_index),
            pl.BlockSpec((TOKEN_TILE, 2048), tok),
            pl.BlockSpec((TOKEN_TILE, 1536), tok),
            pl.BlockSpec((TOKEN_TILE, 3072), tok),
        ],
        out_shape=[
            jax.ShapeDtypeStruct((path.s5_blocks * path.seq, SUBLANES * S5_WIDTH), F32),
            jax.ShapeDtypeStruct((rows, 2048), F32),
            jax.ShapeDtypeStruct((rows, 1536), F32),
            jax.ShapeDtypeStruct((rows, 3072), F32),
        ],
        compiler_params=_cparams(1),
        name="in_proj_lat" if path.latent else "in_proj_ctx",
    )(x, mod, norm_g, w_in)
    return outs


S5_CHUNK = 512


def _s5_kernel(seq, cols, u_ref, b_ref, c_ref, lam_ref, h0_ref, y_ref, hfin_ref, st_ref):
    rows = seq * SUBLANES
    n_chunk = rows // S5_CHUNK
    for d in range(2):
        def expand(ci, carry):
            r0 = pl.multiple_of(ci * S5_CHUNK, S5_CHUNK)
            st_ref[pl.ds(r0, S5_CHUNK), :] = _dot(u_ref[pl.ds(r0, S5_CHUNK), :], b_ref[d, 0])
            return carry
        lax.fori_loop(0, n_chunk, expand, 0)

        lr = lam_ref[d, 0, 0]
        li = lam_ref[d, 0, 1]

        def step(i, carry):
            hr, hi = carry
            t = i if d == 0 else seq - 1 - i
            r0 = pl.multiple_of(t * SUBLANES, SUBLANES)
            br = st_ref[pl.ds(r0, SUBLANES), 0:cols]
            bi = st_ref[pl.ds(r0, SUBLANES), cols:2 * cols]
            nr = lr * hr - li * hi + br
            ni = lr * hi + li * hr + bi
            st_ref[pl.ds(r0, SUBLANES), 0:cols] = nr
            st_ref[pl.ds(r0, SUBLANES), cols:2 * cols] = ni
            return nr, ni
        hr, hi = lax.fori_loop(0, seq, step, (h0_ref[0, d, 0, 0], h0_ref[0, d, 0, 1]), unroll=4)
        hfin_ref[0, d, 0, 0] = hr
        hfin_ref[0, d, 0, 1] = hi

        def contract(ci, carry):
            r0 = pl.multiple_of(ci * S5_CHUNK, S5_CHUNK)
            y = _dot(st_ref[pl.ds(r0, S5_CHUNK), :], c_ref[d, 0])
            if d == 0:
                y_ref[pl.ds(r0, S5_CHUNK), :] = y
            else:
                y_ref[pl.ds(r0, S5_CHUNK), :] += y
            return carry
        lax.fori_loop(0, n_chunk, contract, 0)


def _s5(path, u_tm, b_mat, c_mat, lam, h0):
    nb, rows, cols = path.s5_blocks, path.s5_rows, path.s5_cols
    n_gb = S5_WIDTH // S5_BLOCK_CH
    st_shape = (nb, 2, n_gb, 2, SUBLANES, cols)
    st_spec = pl.BlockSpec((1, 2, 1, 2, SUBLANES, cols), lambda b, g: (b, 0, g, 0, 0, 0))
    return pl.pallas_call(
        functools.partial(_s5_kernel, path.seq, cols),
        grid=(nb, n_gb),
        in_specs=[
            pl.BlockSpec((rows, S5_BLOCK_CH), lambda b, g: (b, g)),
            pl.BlockSpec((2, 1, S5_BLOCK_CH, 2 * cols), lambda b, g: (0, g, 0, 0)),
            pl.BlockSpec((2, 1, 2 * cols, S5_BLOCK_CH), lambda b, g: (0, g, 0, 0)),
            pl.BlockSpec((2, 1, 2, SUBLANES, cols), lambda b, g: (0, g, 0, 0, 0)),
            st_spec,
        ],
        out_specs=[pl.BlockSpec((rows, S5_BLOCK_CH), lambda b, g: (b, g)), st_spec],
        out_shape=[jax.ShapeDtypeStruct((nb * rows, S5_WIDTH), F32), jax.ShapeDtypeStruct(st_shape, F32)],
        scratch_shapes=[pltpu.VMEM((rows, 2 * cols), F32)],
        compiler_params=_cparams(2),
        name="s5_lat" if path.latent else "s5_ctx",
    )(u_tm, b_mat, c_mat, lam, h0)


def _s5_params(path, lam_re, lam_im, log_dt, b_re, b_im, c_re, c_im):
    lam = lax.complex(lam_re.astype(F32), lam_im.astype(F32))
    dt = jnp.exp(log_dt.astype(F32))[..., None]
    lam_bar = jnp.exp(lam * dt)
    b = lax.complex(b_re.astype(F32), b_im.astype(F32))
    b_bar = ((lam_bar - 1.0) / lam)[..., None] * b
    c = lax.complex(c_re.astype(F32), c_im.astype(F32))
    n_gb = S5_WIDTH // S5_BLOCK_CH
    g8 = S5_BLOCK_GROUPS
    n_fold = path.s5_cols // S5_STATE
    fold = (np.arange(g8)[:, None] % n_fold == np.arange(n_fold)[None, :]).astype(np.float32)
    bb = b_bar.reshape(2, n_gb, g8, S5_STATE, S5_GROUP_CH)
    cc = c.reshape(2, n_gb, g8, S5_GROUP_CH, S5_STATE)

    def expand_b(part):
        return jnp.einsum('rbgpc,gh->rbgchp', part, fold).reshape(2, n_gb, S5_BLOCK_CH, path.s5_cols)

    def expand_c(part):
        return jnp.einsum('rbgcp,gh->rbhpgc', part, fold).reshape(2, n_gb, path.s5_cols, S5_BLOCK_CH)

    b_mat = jnp.concatenate([expand_b(jnp.real(bb)), expand_b(jnp.imag(bb))], axis=-1).astype(BF16)
    c_mat = jnp.concatenate([expand_c(jnp.real(cc)), -expand_c(jnp.imag(cc))], axis=-2).astype(BF16)
    lb = lam_bar.reshape(2, n_gb, g8 // n_fold, path.s5_cols)
    lb = jnp.broadcast_to(lb[:, :, None], (2, n_gb, SUBLANES // lb.shape[2], lb.shape[2], path.s5_cols))
    lb = lb.reshape(2, n_gb, SUBLANES, path.s5_cols)
    lam_t = jnp.stack([jnp.real(lb), jnp.imag(lb)], axis=2)
    return b_mat, c_mat, lam_t


def _ret_kernel(seq, latent, n_q, lg_ref, q_ref, k_ref, v_ref, g_ref, *rest):
    if latent:
        cq_ref, sq_ref, ck_ref, sk_ref, s0_ref, o_ref = rest
    else:
        o_ref, st_ref = rest
    tq = TOKEN_TILE
    q0 = (pl.program_id(0) % n_q) * tq
    scale = RET_DK ** -0.5
    tpos = q0 + lax.broadcasted_iota(jnp.int32, (tq, seq), 0)
    spos = lax.broadcasted_iota(jnp.int32, (tq, seq), 1)
    diff = (tpos - spos).astype(F32)
    trow = (q0 + lax.broadcasted_iota(jnp.int32, (tq, RET_DK), 0)).astype(F32)
    srow = lax.broadcasted_iota(jnp.int32, (seq, RET_DK), 0).astype(F32)
    for h in range(RET_HEADS):
        sl = slice(h * RET_DK, (h + 1) * RET_DK)
        lgf = lg_ref[0, h]
        lgb = lg_ref[1, h]
        q = q_ref[:, sl]
        k = k_ref[:, sl]
        v = v_ref[:, sl]
        if latent:
            qr = q * cq_ref[...] + pltpu.roll(q, RET_DK // 2, axis=1) * sq_ref[...]
            kr = k * ck_ref[...] + pltpu.roll(k, RET_DK // 2, axis=1) * sk_ref[...]
        else:
            qr, kr = q, k
        scores = lax.dot_general(qr.astype(BF16), kr.astype(BF16), (((1,), (1,)), ((), ())),
                                 preferred_element_type=F32)
        decay = jnp.exp(jnp.where(diff >= 0, lgf * diff, -lgb * diff)) * scale
        out = _dot(scores * decay, v)
        if latent:
            out = out + _dot(q, s0_ref[0, 0, h]) * jnp.exp(lgf * (trow + 1.0))
            out = out + _dot(q, s0_ref[0, 1, h]) * jnp.exp(lgb * (seq - 1.0 - trow))
        else:
            kf = (k * (jnp.exp(lgf * (seq - 1.0 - srow)) * scale)).astype(BF16)
            kb = (k * (jnp.exp(lgb * srow) * scale)).astype(BF16)
            vb = v.astype(BF16)
            tn = (((0,), (0,)), ((), ()))
            st_ref[0, 0, h] = lax.dot_general(kf, vb, tn, preferred_element_type=F32)
            st_ref[0, 1, h] = lax.dot_general(kb, vb, tn, preferred_element_type=F32)
        xc = out - jnp.mean(out, axis=-1, keepdims=True)
        nrm = xc * lax.rsqrt(jnp.mean(xc * xc, axis=-1, keepdims=True) + GN_EPS)
        g = g_ref[:, sl]
        o_ref[:, sl] = nrm * (g * jax.nn.sigmoid(g))


def _rope_tables(seq):
    n_rows = seq // GRID_W
    rows = jnp.repeat(jnp.arange(n_rows, dtype=F32), GRID_W)
    cols = jnp.tile(jnp.arange(GRID_W, dtype=F32), n_rows)
    half = RET_DK // 2
    n_freq = half // 2
    inv = ROPE_BASE ** (-jnp.arange(n_freq, dtype=F32) / n_freq)
    ang = jnp.concatenate([rows[:, None] * inv, cols[:, None] * inv], axis=-1)
    cos = jnp.cos(ang)
    sin = jnp.sin(ang)
    return jnp.concatenate([cos, cos], axis=-1), jnp.concatenate([-sin, sin], axis=-1)


def _retention(path, qkvg, log_g, s0=None):
    seq, n_q = path.seq, path.tiles_per_seq
    rows = qkvg.shape[0]
    in_specs = [
        pl.BlockSpec(memory_space=pltpu.SMEM),
        pl.BlockSpec((TOKEN_TILE, RET_WIDTH), lambda i: (i, 0)),
        pl.BlockSpec((seq, RET_WIDTH), lambda i: (i // n_q, 1)),
        pl.BlockSpec((seq, RET_WIDTH), lambda i: (i // n_q, 2)),
        pl.BlockSpec((TOKEN_TILE, RET_WIDTH), lambda i: (i, 3)),
    ]
    args = [log_g, qkvg, qkvg, qkvg, qkvg]
    out_specs = [pl.BlockSpec((TOKEN_TILE, RET_WIDTH), lambda i: (i, 0))]
    out_shape = [jax.ShapeDtypeStruct((rows, RET_WIDTH), F32)]
    if path.latent:
        cos, sin = _rope_tables(seq)
        in_specs += [
            pl.BlockSpec((TOKEN_TILE, RET_DK), lambda i: (i % n_q, 0)),
            pl.BlockSpec((TOKEN_TILE, RET_DK), lambda i: (i % n_q, 0)),
            _const_spec((seq, RET_DK)),
            _const_spec((seq, RET_DK)),
            pl.BlockSpec((1, 2, RET_HEADS, RET_DK, RET_DK), lambda i: (i // n_q, 0, 0, 0, 0)),
        ]
        args += [cos, sin, cos, sin, s0]
    else:
        out_specs.append(pl.BlockSpec((1, 2, RET_HEADS, RET_DK, RET_DK), lambda i: (i, 0, 0, 0, 0)))
        out_shape.append(jax.ShapeDtypeStruct((path.batch, 2, RET_HEADS, RET_DK, RET_DK), F32))
    return pl.pallas_call(
        functools.partial(_ret_kernel, seq, path.latent, n_q),
        grid=(path.n_tiles,),
        in_specs=in_specs,
        out_specs=out_specs,
        out_shape=out_shape,
        compiler_params=_cparams(1),
        name="retention_lat" if path.latent else "retention_ctx",
    )(*args)


def _hyena_kernel(seq, x1_ref, x2_ref, v_ref, w1_ref, w2_ref, wv_ref, b1_ref, b2_ref, bv_ref,
                  f_ref, g_ref, p_ref, pn_ref, q_ref, bias_ref, o_ref):
    cn = o_ref.shape[1]
    row = lax.broadcasted_iota(jnp.int32, (seq, cn), 0)

    def short_conv(x_ref, w_ref, b_ref):
        x = x_ref[...]
        prev = jnp.where(row == 0, 0.0, pltpu.roll(x, 1, axis=0))
        nxt = jnp.where(row == seq - 1, 0.0, pltpu.roll(x, seq - 1, axis=0))
        return prev * w_ref[0:1, :] + x * w_ref[1:2, :] + nxt * w_ref[2:3, :] + b_ref[...]

    out = short_conv(v_ref, wv_ref, bv_ref)
    for o, (x_ref, w_ref, b_ref) in enumerate(((x1_ref, w1_ref, b1_ref), (x2_ref, w2_ref, b2_ref))):
        z = _dot(f_ref[...], out)
        zr, zi = z[:seq], z[seq:]
        yr = zr * p_ref[o] - zi * q_ref[o]
        yi = zr * q_ref[o] + zi * pn_ref[o]
        y = jnp.concatenate([yr.astype(BF16), yi.astype(BF16)], axis=0)
        conv = _dot(g_ref[...], y)
        out = short_conv(x_ref, w_ref, b_ref) * (conv + bias_ref[o:o + 1, :] * out)
    o_ref[...] = out


def _dft_tables(seq):
    n = 2 * seq
    k = lax.broadcasted_iota(jnp.int32, (seq, seq), 0)
    t = lax.broadcasted_iota(jnp.int32, (seq, seq), 1)
    ang = ((k * t) % n).astype(F32) * (2.0 * math.pi / n)
    cos, sin = jnp.cos(ang), jnp.sin(ang)
    alt = jnp.where(t % 2 == 0, 1.0, -1.0)
    fwd = jnp.concatenate([cos, jnp.where(k == 0, alt, -sin)], axis=0)
    wk = jnp.where(k == 0, 1.0, 2.0) / n
    inv_c = (wk * cos).T
    inv_s = jnp.where(k == 0, alt / n, -(2.0 / n) * sin).T
    inv = jnp.concatenate([inv_c, inv_s], axis=1)
    return fwd.astype(BF16), inv.astype(BF16)


def _hyena_spectra(seq, hy_w1, hy_b1, hy_w2, hy_b2, hy_freq, hy_w3):
    t = jnp.arange(seq, dtype=F32)
    t_norm = t / seq
    bands = jnp.linspace(1e-4, HY_BANDS - 1, HY_BANDS, dtype=F32)
    ang = (2.0 * math.pi / seq) * t[:, None] * bands[None, :]
    z = jnp.concatenate([t_norm[:, None], jnp.cos(ang), -jnp.sin(ang)], axis=-1)
    freq = hy_freq.astype(F32)
    hid = jnp.sin(freq[0] * (z @ hy_w1.astype(F32) + hy_b1.astype(F32)))
    hid = jnp.sin(freq[1] * (hid @ hy_w2.astype(F32) + hy_b2.astype(F32)))
    filt = (hid @ hy_w3.astype(F32)).reshape(seq, 2, HY_ORDER, HY_WIDTH)
    rate = jnp.linspace(HY_DECAY_MIN, HY_DECAY_MAX, HY_WIDTH, dtype=F32)
    filt = filt * jnp.exp(-t_norm[:, None, None, None] * rate)
    fwd, bwd = filt[:, 0], filt[:, 1]
    kern = jnp.concatenate([fwd, jnp.zeros((1, HY_ORDER, HY_WIDTH), F32), bwd[:0:-1]], axis=0)
    kern = kern * lax.rsqrt(jnp.sum(kern * kern, axis=0, keepdims=True) + EPS)
    spec = jnp.fft.rfft(kern, axis=0)
    p = jnp.transpose(jnp.real(spec[:seq]), (1, 0, 2))
    q = jnp.transpose(jnp.imag(spec[:seq]), (1, 0, 2)).at[:, 0, :].set(0.0)
    pn = p.at[:, 0, :].set(jnp.real(spec[seq]))
    return p, pn, q


def _hyena(path, hy, conv_w, conv_b, tables, spectra, bias):
    seq = path.seq
    cn = 512 if seq <= 256 else 256
    nj = HY_WIDTH // cn
    fwd, inv = tables
    p, pn, q = spectra
    rows = hy.shape[0]
    conv_b = conv_b.reshape(1, 3 * HY_WIDTH)

    def col(part):
        return lambda j, b: (b, part * nj + j)

    def wcol(part):
        return lambda j, b: (0, part * nj + j)

    spec3 = pl.BlockSpec((HY_ORDER, seq, cn), lambda j, b: (0, 0, j))
    return pl.pallas_call(
        functools.partial(_hyena_kernel, seq),
        grid=(nj, path.batch),
        in_specs=[pl.BlockSpec((seq, cn), col(part)) for part in range(3)]
        + [pl.BlockSpec((3, cn), wcol(part)) for part in range(3)]
        + [pl.BlockSpec((1, cn), wcol(part)) for part in range(3)]
        + [_const_spec(fwd.shape), _const_spec(inv.shape), spec3, spec3, spec3,
           pl.BlockSpec((HY_ORDER, cn), lambda j, b: (0, j))],
        out_specs=pl.BlockSpec((seq, cn), lambda j, b: (b, j)),
        out_shape=jax.ShapeDtypeStruct((rows, HY_WIDTH), F32),
        compiler_params=_cparams(2),
        name="hyena_lat" if path.latent else "hyena_ctx",
    )(hy, hy, hy, conv_w, conv_w, conv_w, conv_b, conv_b, conv_b, fwd, inv, p, pn, q, bias)


def _merge_kernel(latent, x_ref, mod_ref, y_ref, u_ref, d_ref, ret_ref, hy_ref, gl_ref,
                  wglu_ref, wret_ref, why_ref, wout_ref, o_ref):
    if latent:
        m0 = _half_mask()
        y = jnp.where(m0, y_ref[:, 0:S5_WIDTH], y_ref[:, S5_WIDTH:])
        u = u_ref[:, 0:S5_WIDTH] + u_ref[:, S5_WIDTH:]
    else:
        y, u = y_ref[...], u_ref[...]
    y_s5 = y + d_ref[...] * u
    ab = _dot(jax.nn.gelu(y_s5), wglu_ref[...])
    br_s5 = ab[:, :D_MODEL] * jax.nn.sigmoid(ab[:, D_MODEL:])
    br_ret = _dot(ret_ref[...], wret_ref[...])
    br_hy = _dot(hy_ref[...], why_ref[...])
    gates = jax.nn.sigmoid(gl_ref[...])
    merged = (gates[:, 0:D_MODEL] * br_s5 + gates[:, D_MODEL:2 * D_MODEL] * br_ret
              + gates[:, 2 * D_MODEL:] * br_hy)
    mix = _dot(merged, wout_ref[...])
    o_ref[...] = x_ref[...] + mod_ref[0, 2:3, :] * mix


def _merge(path, x, mod, y_tm, u_tm, d, ret, hyo, gl, w_glu, w_ret, w_hy, w_out):
    rows = x.shape[0]
    tok = lambda i: (i, 0)
    tm_spec = pl.BlockSpec((TOKEN_TILE, path.tm_width), path.tm_index)
    return pl.pallas_call(
        functools.partial(_merge_kernel, path.latent),
        grid=(path.n_tiles,),
        in_specs=[
            pl.BlockSpec((TOKEN_TILE, D_MODEL), tok),
            pl.BlockSpec((1, 6, D_MODEL), lambda i: (path.mod_index(i), 0, 0)),
            tm_spec, tm_spec,
            _const_spec((1, S5_WIDTH)),
            pl.BlockSpec((TOKEN_TILE, RET_WIDTH), tok),
            pl.BlockSpec((TOKEN_TILE, HY_WIDTH), tok),
            pl.BlockSpec((TOKEN_TILE, 3 * D_MODEL), tok),
            _const_spec(w_glu.shape), _const_spec(w_ret.shape), _const_spec(w_hy.shape), _const_spec(w_out.shape),
        ],
        out_specs=pl.BlockSpec((TOKEN_TILE, D_MODEL), tok),
        out_shape=jax.ShapeDtypeStruct((rows, D_MODEL), F32),
        compiler_params=_cparams(1),
        name="merge_lat" if path.latent else "merge_ctx",
    )(x, mod, y_tm, u_tm, d, ret, hyo, gl, w_glu, w_ret, w_hy, w_out)


def _ffn_kernel(final, x_ref, mod_ref, g_ref, win_ref, wout_ref, gf_ref, o_ref):
    x = x_ref[...]
    h = _rms_mod(x, g_ref[...], mod_ref[0, 4:5, :], mod_ref[0, 3:4, :]).astype(BF16)
    a = jnp.dot(h, win_ref[:, :D_FF], preferred_element_type=F32)
    b = jnp.dot(h, win_ref[:, D_FF:], preferred_element_type=F32)
    act = (a * jax.nn.sigmoid(a)) * b
    x = x + mod_ref[0, 5:6, :] * _dot(act, wout_ref[...])
    if final:
        x = (x * lax.rsqrt(jnp.mean(x * x, axis=-1, keepdims=True) + EPS)) * gf_ref[...]
    o_ref[...] = x


def _ffn(path, x, mod, norm_g, w_in, w_out, norm_f, final):
    rows = x.shape[0]
    tok = lambda i: (i, 0)
    return pl.pallas_call(
        functools.partial(_ffn_kernel, final),
        grid=(path.n_tiles,),
        in_specs=[
            pl.BlockSpec((TOKEN_TILE, D_MODEL), tok),
            pl.BlockSpec((1, 6, D_MODEL), lambda i: (path.mod_index(i), 0, 0)),
            _const_spec((1, D_MODEL)),
            _const_spec(w_in.shape), _const_spec(w_out.shape),
            _const_spec((1, D_MODEL)),
        ],
        out_specs=pl.BlockSpec((TOKEN_TILE, D_MODEL), tok),
        out_shape=jax.ShapeDtypeStruct((rows, D_MODEL), F32),
        compiler_params=_cparams(1),
        name="ffn_lat" if path.latent else "ffn_ctx",
    )(x, mod, norm_g, w_in, w_out, norm_f)


def kernel(x_prompt, x_sample, state_s5, state_ret, c, c_ctx, w_mod, b_mod, norm1, norm2, w_in, s5_lam_re, s5_lam_im, s5_log_dt, s5_b_re, s5_b_im, s5_c_re, s5_c_im, s5_d, w_s5_glu, ret_decay, w_ret_o, hy_conv_w, hy_conv_b, hy_w1, hy_b1, hy_w2, hy_b2, hy_freq, hy_w3, hy_bias, w_hy_o, w_out, w_ffn_in, w_ffn_out, norm_f):
    batch, seq, _ = x_prompt.shape
    dec_batch, dec_seq, _ = x_sample.shape
    ctx = _Path(batch, seq, latent=False)
    lat = _Path(dec_batch, dec_seq, latent=True)
    n_gb = S5_WIDTH // S5_BLOCK_CH

    c_rows = jnp.concatenate([c_ctx[None, :], c, jnp.zeros((SUBLANES - 1 - dec_batch, D_MODEL), F32)], axis=0)
    mod_all = _modulation(c_rows.astype(F32), w_mod.astype(BF16), b_mod.astype(F32))
    mod_all = mod_all.reshape(DEPTH, SUBLANES, 6, D_MODEL)

    tables = {p: _dft_tables(p.seq) for p in (ctx, lat)}
    log_g_all = jnp.log1p(-jnp.exp(ret_decay.astype(F32)))
    norm_f2 = norm_f.astype(F32).reshape(1, D_MODEL)

    xs = {ctx: x_prompt.astype(F32).reshape(batch * seq, D_MODEL),
          lat: x_sample.astype(F32).reshape(dec_batch * dec_seq, D_MODEL)}
    s5_states, ret_states = [], []
    for l in range(DEPTH):
        mod = mod_all[l]
        w_in_l = w_in[l].astype(BF16)
        w_glu_l, w_ret_l = w_s5_glu[l].astype(BF16), w_ret_o[l].astype(BF16)
        w_hy_l, w_out_l = w_hy_o[l].astype(BF16), w_out[l].astype(BF16)
        w_fin_l, w_fout_l = w_ffn_in[l].astype(BF16), w_ffn_out[l].astype(BF16)
        d_l = s5_d[l].astype(F32).reshape(1, S5_WIDTH)
        n1 = norm1[l].astype(F32).reshape(1, D_MODEL)
        n2 = norm2[l].astype(F32).reshape(1, D_MODEL)
        for path in (ctx, lat):
            x = xs[path]
            u_tm, qkvg, hy, gl = _in_proj(path, x, mod, n1, w_in_l)
            u_tm = u_tm.reshape(path.s5_blocks * path.s5_rows, S5_WIDTH)

            b_mat, c_mat, lam_t = _s5_params(path, s5_lam_re[l], s5_lam_im[l], s5_log_dt[l],
                                             s5_b_re[l], s5_b_im[l], s5_c_re[l], s5_c_im[l])
            if path.latent:
                st = state_s5[:, l].astype(F32).reshape(dec_batch, 2, n_gb, 2, path.s5_cols, 2)
                h0 = jnp.transpose(st, (1, 2, 5, 0, 3, 4)).reshape(1, 2, n_gb, 2, SUBLANES, path.s5_cols)
            else:
                h0 = jnp.zeros((path.s5_blocks, 2, n_gb, 2, SUBLANES, path.s5_cols), F32)
            y_tm, h_fin = _s5(path, u_tm, b_mat, c_mat, lam_t, h0)

            if path.latent:
                ret, = _retention(path, qkvg, log_g_all[l], state_ret[:, l].astype(F32))
            else:
                ret, r_state = _retention(path, qkvg, log_g_all[l])
                ret_states.append(r_state)
                hf = h_fin.reshape(path.s5_blocks, 2, n_gb, 2, SUBLANES, S5_BLOCK_GROUPS, S5_STATE)
                hf = jnp.transpose(hf, (0, 4, 1, 2, 5, 6, 3)).reshape(batch, 2, S5_GROUPS, S5_STATE, 2)
                s5_states.append(hf)

            spectra = _hyena_spectra(path.seq, hy_w1[l], hy_b1[l], hy_w2[l], hy_b2[l], hy_freq[l], hy_w3[l])
            hyo = _hyena(path, hy, hy_conv_w[l].astype(F32), hy_conv_b[l].astype(F32), tables[path], spectra,
                         hy_bias[l].astype(F32))

            tm_view = (path.s5_blocks * path.seq, SUBLANES * S5_WIDTH)
            x = _merge(path, x, mod, y_tm.reshape(tm_view), u_tm.reshape(tm_view), d_l, ret, hyo, gl,
                       w_glu_l, w_ret_l, w_hy_l, w_out_l)
            xs[path] = _ffn(path, x, mod, n2, w_fin_l, w_fout_l, norm_f2, final=(l == DEPTH - 1))

    y_prompt = xs[ctx].reshape(batch, seq, D_MODEL).astype(x_prompt.dtype)
    y_sample = xs[lat].reshape(dec_batch, dec_seq, D_MODEL).astype(x_sample.dtype)
    new_state_s5 = jnp.stack(s5_states, axis=1).astype(x_prompt.dtype)
    new_state_ret = jnp.stack(ret_states, axis=1).astype(x_prompt.dtype)
    return (y_prompt, y_sample, new_state_s5, new_state_ret)
```

```python
import functools
import math

import jax
import jax.numpy as jnp
import numpy as np
from jax import lax
from jax.experimental import pallas as pl
from jax.experimental.pallas import tpu as pltpu

F32 = jnp.float32
BF16 = jnp.bfloat16

D_MODEL = 1024
DEPTH = 2
GRID_W = 64
EPS = 1e-6
GN_EPS = 1e-5
S5_WIDTH = 512
S5_GROUP_CH = 16
S5_GROUPS = 32
S5_STATE = 64
RET_WIDTH = 512
RET_HEADS = 4
RET_DK = 128
ROPE_BASE = 10000.0
HY_WIDTH = 512
HY_ORDER = 2
HY_BANDS = 16
HY_EMB = 1 + 2 * HY_BANDS
HY_HIDDEN = 64
HY_DECAY_MIN = -math.log(1e-2) / 1.5
HY_DECAY_MAX = -math.log(1e-2) / 0.3
D_FF = 2816

LANES = 128
SUBLANES = 8
TOKEN_TILE = 256
S5_BLOCK_CH = LANES
S5_BLOCK_GROUPS = S5_BLOCK_CH // S5_GROUP_CH
S5_N_BLOCKS = S5_WIDTH // S5_BLOCK_CH
VMEM_LIMIT = 56 * 1024 * 1024


def _cparams(n_axes):
    return pltpu.CompilerParams(dimension_semantics=("arbitrary",) * n_axes, vmem_limit_bytes=VMEM_LIMIT)


def _const_spec(shape):
    nd = len(shape)
    return pl.BlockSpec(shape, lambda *_: (0,) * nd, pipeline_mode=pl.Buffered(1))


def _dot(a, b):
    return jnp.dot(a.astype(BF16), b.astype(BF16), preferred_element_type=F32)


def _split(a):
    hi = a.astype(BF16)
    return hi, (a - hi.astype(F32)).astype(BF16)


def _dot3(a, b):
    a_hi, a_lo = _split(a)
    b_hi, b_lo = _split(b)
    dot = functools.partial(jnp.dot, preferred_element_type=F32)
    return dot(a_hi, b_hi) + (dot(a_hi, b_lo) + dot(a_lo, b_hi))


def _rms_mod(x, g, scale, shift):
    y = x * lax.rsqrt(jnp.mean(x * x, axis=-1, keepdims=True) + EPS)
    return (y * g) * (1.0 + scale) + shift


def _mod_kernel(c_ref, w_ref, b_ref, o_ref):
    c = c_ref[...]
    cond = c * jax.nn.sigmoid(c)
    o_ref[0] = _dot(cond, w_ref[0]) + b_ref[0]


def _modulation(c_rows, w_mod, b_mod):
    n_chunk = 6
    return pl.pallas_call(
        _mod_kernel,
        grid=(DEPTH, n_chunk),
        in_specs=[
            pl.BlockSpec((SUBLANES, D_MODEL), lambda l, j: (0, 0)),
            pl.BlockSpec((1, D_MODEL, D_MODEL), lambda l, j: (l, 0, j)),
            pl.BlockSpec((1, 1, D_MODEL), lambda l, j: (l, 0, j)),
        ],
        out_specs=pl.BlockSpec((1, SUBLANES, D_MODEL), lambda l, j: (l, 0, j)),
        out_shape=jax.ShapeDtypeStruct((DEPTH, SUBLANES, 6 * D_MODEL), F32),
        compiler_params=_cparams(2),
        name="modulation",
    )(c_rows, w_mod, b_mod.reshape(DEPTH, 1, 6 * D_MODEL))


class _Path:
    def __init__(self, batch, seq, latent):
        self.batch = batch
        self.seq = seq
        self.latent = latent
        self.tiles_per_seq = seq // TOKEN_TILE
        self.n_tiles = batch * self.tiles_per_seq
        if latent:
            assert batch * 2 == SUBLANES
            self.s5_blocks = 1
            self.s5_cols = S5_BLOCK_GROUPS * S5_STATE // 2
            self.tiles_per_tm = batch
        else:
            assert batch % SUBLANES == 0 and seq == TOKEN_TILE
            self.s5_blocks = batch // SUBLANES
            self.s5_cols = S5_BLOCK_GROUPS * S5_STATE
            self.tiles_per_tm = SUBLANES
        self.s5_rows = seq * SUBLANES
        self.tm_rows = TOKEN_TILE * SUBLANES

    def tok(self, i):
        if self.latent:
            return (i % self.batch) * self.tiles_per_seq + i // self.batch
        return i

    def tok_index(self, i):
        return (self.tok(i), 0)

    def mod_index(self, i):
        return (1 + i % self.batch if self.latent else 0, 0, 0)

    def tm_index(self, i):
        return (0, i // self.tiles_per_tm, 0)

    def tm_spec(self):
        return pl.BlockSpec((S5_N_BLOCKS, self.tm_rows, S5_BLOCK_CH), self.tm_index)

    def tm_shape(self):
        return jax.ShapeDtypeStruct((S5_N_BLOCKS, self.s5_blocks * self.s5_rows, S5_BLOCK_CH), F32)

    def row_in_step(self):
        r = pl.program_id(0) % self.tiles_per_tm
        return 2 * r if self.latent else r


def _half_mask(width):
    ch = lax.broadcasted_iota(jnp.int32, (1, width), 1)
    return (ch % S5_BLOCK_CH) < (S5_BLOCK_CH // 2)


def _tm_rows(ref, blk, r):
    return ref.at[blk, pl.ds(r, TOKEN_TILE, stride=SUBLANES), :]


_IN_SEGS = ((0, 512), (512, 2560), (2560, 4096), (4096, 7168))


def _in_proj_kernel(path, x_ref, mod_ref, g_ref, w_ref, u_ref, qkvg_ref, hy_ref, gl_ref):
    h = _rms_mod(x_ref[...], g_ref[...], mod_ref[0, 1:2, :], mod_ref[0, 0:1, :]).astype(BF16)
    u = jnp.dot(h, w_ref[:, 0:512], preferred_element_type=F32)
    r = path.row_in_step()
    m0 = _half_mask(S5_BLOCK_CH)
    for blk in range(S5_N_BLOCKS):
        ub = u[:, blk * S5_BLOCK_CH:(blk + 1) * S5_BLOCK_CH]
        if path.latent:
            _tm_rows(u_ref, blk, r)[...] = jnp.where(m0, ub, 0.0)
            _tm_rows(u_ref, blk, r + 1)[...] = jnp.where(m0, 0.0, ub)
        else:
            _tm_rows(u_ref, blk, r)[...] = ub
    for ref, (lo, hi) in zip((qkvg_ref, hy_ref, gl_ref), _IN_SEGS[1:]):
        ref[...] = jnp.dot(h, w_ref[:, lo:hi], preferred_element_type=F32).astype(ref.dtype)


def _in_proj(path, x, mod, norm_g, w_in):
    rows = x.shape[0]
    return pl.pallas_call(
        functools.partial(_in_proj_kernel, path),
        grid=(path.n_tiles,),
        in_specs=[
            pl.BlockSpec((TOKEN_TILE, D_MODEL), path.tok_index),
            pl.BlockSpec((1, 6, D_MODEL), path.mod_index),
            _const_spec((1, D_MODEL)),
            _const_spec(w_in.shape),
        ],
        out_specs=[
            path.tm_spec(),
            pl.BlockSpec((TOKEN_TILE, 2048), path.tok_index),
            pl.BlockSpec((TOKEN_TILE, 1536), path.tok_index),
            pl.BlockSpec((TOKEN_TILE, 3072), path.tok_index),
        ],
        out_shape=[
            path.tm_shape(),
            jax.ShapeDtypeStruct((rows, 2048), BF16),
            jax.ShapeDtypeStruct((rows, 1536), BF16),
            jax.ShapeDtypeStruct((rows, 3072), BF16),
        ],
        compiler_params=_cparams(1),
        name="in_proj_lat" if path.latent else "in_proj_ctx",
    )(x, mod, norm_g, w_in)


S5_CHUNK = 512


def _s5_kernel(seq, cols, u_ref, b_ref, c_ref, lam_ref, h0_ref, y_ref, hfin_ref, st_ref):
    rows = seq * SUBLANES
    n_chunk = rows // S5_CHUNK
    for d in range(2):
        def expand(ci, carry):
            r0 = pl.multiple_of(ci * S5_CHUNK, S5_CHUNK)
            st_ref[pl.ds(r0, S5_CHUNK), :] = _dot(u_ref[pl.ds(r0, S5_CHUNK), :], b_ref[d, 0])
            return carry
        lax.fori_loop(0, n_chunk, expand, 0)

        lr = lam_ref[d, 0, 0]
        li = lam_ref[d, 0, 1]

        def step(i, carry):
            hr, hi = carry
            t = i if d == 0 else seq - 1 - i
            r0 = pl.multiple_of(t * SUBLANES, SUBLANES)
            br = st_ref[pl.ds(r0, SUBLANES), 0:cols]
            bi = st_ref[pl.ds(r0, SUBLANES), cols:2 * cols]
            nr = lr * hr - li * hi + br
            ni = lr * hi + li * hr + bi
            st_ref[pl.ds(r0, SUBLANES), 0:cols] = nr
            st_ref[pl.ds(r0, SUBLANES), cols:2 * cols] = ni
            return nr, ni
        hr, hi = lax.fori_loop(0, seq, step, (h0_ref[0, d, 0, 0], h0_ref[0, d, 0, 1]), unroll=4)
        hfin_ref[0, d, 0, 0] = hr
        hfin_ref[0, d, 0, 1] = hi

        def contract(ci, carry):
            r0 = pl.multiple_of(ci * S5_CHUNK, S5_CHUNK)
            y = _dot(st_ref[pl.ds(r0, S5_CHUNK), :], c_ref[d, 0])
            if d == 0:
                y_ref[pl.ds(r0, S5_CHUNK), :] = y
            else:
                y_ref[pl.ds(r0, S5_CHUNK), :] += y
            return carry
        lax.fori_loop(0, n_chunk, contract, 0)


def _s5(path, layer, u_tm, b_mat, c_mat, lam, h0):
    nb, rows, cols = path.s5_blocks, path.s5_rows, path.s5_cols
    st_shape = (nb, 2, S5_N_BLOCKS, 2, SUBLANES, cols)
    st_spec = pl.BlockSpec((1, 2, 1, 2, SUBLANES, cols), lambda b, g: (b, 0, g, 0, 0, 0))
    tm_spec = pl.BlockSpec((None, rows, S5_BLOCK_CH), lambda b, g: (g, b, 0))
    return pl.pallas_call(
        functools.partial(_s5_kernel, path.seq, cols),
        grid=(nb, S5_N_BLOCKS),
        in_specs=[
            tm_spec,
            pl.BlockSpec((None, 2, 1, S5_BLOCK_CH, 2 * cols), lambda b, g: (layer, 0, g, 0, 0)),
            pl.BlockSpec((None, 2, 1, 2 * cols, S5_BLOCK_CH), lambda b, g: (layer, 0, g, 0, 0)),
            pl.BlockSpec((None, 2, 1, 2, SUBLANES, cols), lambda b, g: (layer, 0, g, 0, 0, 0)),
            st_spec,
        ],
        out_specs=[tm_spec, st_spec],
        out_shape=[path.tm_shape(), jax.ShapeDtypeStruct(st_shape, F32)],
        scratch_shapes=[pltpu.VMEM((rows, 2 * cols), F32)],
        compiler_params=_cparams(2),
        name="s5_lat" if path.latent else "s5_ctx",
    )(u_tm, b_mat, c_mat, lam, h0)


def _s5_params(lam_re, lam_im, log_dt, b_re, b_im, c_re, c_im):
    lam = lax.complex(lam_re.astype(F32), lam_im.astype(F32))
    dt = jnp.exp(log_dt.astype(F32))[..., None]
    lam_bar = jnp.exp(lam * dt)
    b = lax.complex(b_re.astype(F32), b_im.astype(F32))
    b_bar = ((lam_bar - 1.0) / lam)[..., None] * b
    c = lax.complex(c_re.astype(F32), c_im.astype(F32))
    g8, nblk = S5_BLOCK_GROUPS, S5_N_BLOCKS
    cols = g8 * S5_STATE
    eye = np.eye(g8, dtype=np.float32)
    bb = b_bar.reshape(DEPTH, 2, nblk, g8, S5_STATE, S5_GROUP_CH)
    cc = c.reshape(DEPTH, 2, nblk, g8, S5_GROUP_CH, S5_STATE)

    def expand_b(part):
        return jnp.einsum('lrbgpc,gh->lrbgchp', part, eye).reshape(DEPTH, 2, nblk, S5_BLOCK_CH, cols)

    def expand_c(part):
        return jnp.einsum('lrbgcp,gh->lrbhpgc', part, eye).reshape(DEPTH, 2, nblk, cols, S5_BLOCK_CH)

    b_r, b_i = expand_b(jnp.real(bb)), expand_b(jnp.imag(bb))
    c_r, c_i = expand_c(jnp.real(cc)), -expand_c(jnp.imag(cc))
    half = cols // 2
    ctx = (jnp.concatenate([b_r, b_i], axis=-1).astype(BF16),
           jnp.concatenate([c_r, c_i], axis=-2).astype(BF16))
    lat = (jnp.concatenate([b_r[..., :half] + b_r[..., half:], b_i[..., :half] + b_i[..., half:]], axis=-1).astype(BF16),
           jnp.concatenate([c_r[..., :half, :] + c_r[..., half:, :], c_i[..., :half, :] + c_i[..., half:, :]],
                           axis=-2).astype(BF16))
    lb = jnp.stack([jnp.real(lam_bar), jnp.imag(lam_bar)], axis=2)
    lam_ctx = jnp.broadcast_to(lb.reshape(DEPTH, 2, 2, nblk, 1, cols), (DEPTH, 2, 2, nblk, SUBLANES, cols))
    lam_lat = jnp.broadcast_to(lb.reshape(DEPTH, 2, 2, nblk, 1, 2, half), (DEPTH, 2, 2, nblk, SUBLANES // 2, 2, half))
    lam_ctx = jnp.transpose(lam_ctx, (0, 1, 3, 2, 4, 5))
    lam_lat = jnp.transpose(lam_lat.reshape(DEPTH, 2, 2, nblk, SUBLANES, half), (0, 1, 3, 2, 4, 5))
    return ctx + (lam_ctx,), lat + (lam_lat,)


def _ret_kernel(seq, latent, n_q, lg_ref, q_ref, k_ref, v_ref, g_ref, *rest):
    if latent:
        cq_ref, sq_ref, ck_ref, sk_ref, s0_ref, o_ref = rest
    else:
        o_ref, st_ref = rest
    tq = TOKEN_TILE
    q0 = (pl.program_id(0) % n_q) * tq
    scale = RET_DK ** -0.5
    tpos = q0 + lax.broadcasted_iota(jnp.int32, (tq, seq), 0)
    spos = lax.broadcasted_iota(jnp.int32, (tq, seq), 1)
    diff = (tpos - spos).astype(F32)
    trow = (q0 + lax.broadcasted_iota(jnp.int32, (tq, RET_DK), 0)).astype(F32)
    srow = lax.broadcasted_iota(jnp.int32, (seq, RET_DK), 0).astype(F32)
    for h in range(RET_HEADS):
        sl = slice(h * RET_DK, (h + 1) * RET_DK)
        lgf = lg_ref[0, h]
        lgb = lg_ref[1, h]
        q = q_ref[:, sl].astype(F32)
        k = k_ref[:, sl].astype(F32)
        v = v_ref[:, sl]
        if latent:
            qr = q * cq_ref[...] + pltpu.roll(q, RET_DK // 2, axis=1) * sq_ref[...]
            kr = k * ck_ref[...] + pltpu.roll(k, RET_DK // 2, axis=1) * sk_ref[...]
        else:
            qr, kr = q, k
        scores = lax.dot_general(qr.astype(BF16), kr.astype(BF16), (((1,), (1,)), ((), ())),
                                 preferred_element_type=F32)
        decay = jnp.exp(jnp.where(diff >= 0, lgf * diff, -lgb * diff)) * scale
        out = _dot(scores * decay, v)
        if latent:
            out = out + _dot(q, s0_ref[0, 0, h]) * jnp.exp(lgf * (trow + 1.0))
            out = out + _dot(q, s0_ref[0, 1, h]) * jnp.exp(lgb * (seq - 1.0 - trow))
        else:
            kf = (k * (jnp.exp(lgf * (seq - 1.0 - srow)) * scale)).astype(BF16)
            kb = (k * (jnp.exp(lgb * srow) * scale)).astype(BF16)
            tn = (((0,), (0,)), ((), ()))
            st_ref[0, 0, h] = lax.dot_general(kf, v, tn, preferred_element_type=F32)
            st_ref[0, 1, h] = lax.dot_general(kb, v, tn, preferred_element_type=F32)
        xc = out - jnp.mean(out, axis=-1, keepdims=True)
        nrm = xc * lax.rsqrt(jnp.mean(xc * xc, axis=-1, keepdims=True) + GN_EPS)
        g = g_ref[:, sl].astype(F32)
        o_ref[:, sl] = (nrm * (g * jax.nn.sigmoid(g))).astype(o_ref.dtype)


def _rope_tables(seq):
    n_rows = seq // GRID_W
    rows = jnp.repeat(jnp.arange(n_rows, dtype=F32), GRID_W)
    cols = jnp.tile(jnp.arange(GRID_W, dtype=F32), n_rows)
    half = RET_DK // 2
    n_freq = half // 2
    inv = ROPE_BASE ** (-jnp.arange(n_freq, dtype=F32) / n_freq)
    ang = jnp.concatenate([rows[:, None] * inv, cols[:, None] * inv], axis=-1)
    cos = jnp.cos(ang)
    sin = jnp.sin(ang)
    return jnp.concatenate([cos, cos], axis=-1), jnp.concatenate([-sin, sin], axis=-1)


def _retention(path, qkvg, log_g, rope=None, s0=None):
    seq, n_q = path.seq, path.tiles_per_seq
    rows = qkvg.shape[0]
    in_specs = [
        pl.BlockSpec(memory_space=pltpu.SMEM),
        pl.BlockSpec((TOKEN_TILE, RET_WIDTH), lambda i: (i, 0)),
        pl.BlockSpec((seq, RET_WIDTH), lambda i: (i // n_q, 1)),
        pl.BlockSpec((seq, RET_WIDTH), lambda i: (i // n_q, 2)),
        pl.BlockSpec((TOKEN_TILE, RET_WIDTH), lambda i: (i, 3)),
    ]
    args = [log_g, qkvg, qkvg, qkvg, qkvg]
    out_specs = [pl.BlockSpec((TOKEN_TILE, RET_WIDTH), lambda i: (i, 0))]
    out_shape = [jax.ShapeDtypeStruct((rows, RET_WIDTH), BF16)]
    if path.latent:
        cos, sin = rope
        in_specs += [
            pl.BlockSpec((TOKEN_TILE, RET_DK), lambda i: (i % n_q, 0)),
            pl.BlockSpec((TOKEN_TILE, RET_DK), lambda i: (i % n_q, 0)),
            _const_spec((seq, RET_DK)),
            _const_spec((seq, RET_DK)),
            pl.BlockSpec((1, 2, RET_HEADS, RET_DK, RET_DK), lambda i: (i // n_q, 0, 0, 0, 0)),
        ]
        args += [cos, sin, cos, sin, s0]
    else:
        out_specs.append(pl.BlockSpec((1, 2, RET_HEADS, RET_DK, RET_DK), lambda i: (i, 0, 0, 0, 0)))
        out_shape.append(jax.ShapeDtypeStruct((path.batch, 2, RET_HEADS, RET_DK, RET_DK), F32))
    return pl.pallas_call(
        functools.partial(_ret_kernel, seq, path.latent, n_q),
        grid=(path.n_tiles,),
        in_specs=in_specs,
        out_specs=out_specs,
        out_shape=out_shape,
        compiler_params=_cparams(1),
        name="retention_lat" if path.latent else "retention_ctx",
    )(*args)


FILTER_COLS = 256


def _dft_tables(seq):
    n = 2 * seq
    k = lax.broadcasted_iota(jnp.int32, (seq, seq), 0)
    t = lax.broadcasted_iota(jnp.int32, (seq, seq), 1)
    ang = ((k * t) % n).astype(F32) * (2.0 * math.pi / n)
    cos, sin = jnp.cos(ang), jnp.sin(ang)
    alt = jnp.where(t % 2 == 0, 1.0, -1.0)
    fwd = jnp.concatenate([cos, jnp.where(k == 0, alt, -sin)], axis=0)
    wk = jnp.where(k == 0, 1.0, 2.0) / n
    inv_c = (wk * cos).T
    inv_s = jnp.where(k == 0, alt / n, -(2.0 / n) * sin).T
    inv = jnp.concatenate([inv_c, inv_s], axis=1)
    fwd_hi = fwd.astype(BF16)
    fwd_lo = (fwd - fwd_hi.astype(F32)).astype(BF16)
    return fwd_hi, fwd_lo, inv.astype(BF16)


def _filter_kernel(seq, z_ref, w1_ref, b1_ref, w2_ref, b2_ref, fr_ref, w3f_ref, w3b_ref, rate_ref,
                   fh_ref, fl_ref, p_ref, q_ref, nyq_ref):
    cb = p_ref.shape[-1]
    hid = jnp.sin(fr_ref[0, 0:1, :] * (_dot3(z_ref[...], w1_ref[0]) + b1_ref[0]))
    hid = jnp.sin(fr_ref[0, 1:2, :] * (_dot3(hid, w2_ref[0]) + b2_ref[0]))
    row = lax.broadcasted_iota(jnp.int32, (seq, cb), 0)
    win = jnp.exp(-(row.astype(F32) / seq) * rate_ref[...])
    fwd = _dot3(hid, w3f_ref[0]) * win
    bwd = jnp.where(row == 0, 0.0, _dot3(hid, w3b_ref[0]) * win)
    scale = lax.rsqrt(jnp.sum(fwd * fwd + bwd * bwd, axis=0, keepdims=True) + EPS)
    even = (fwd + bwd) * scale
    odd = (fwd - bwd) * scale

    def dft(lo_row, x):
        x_hi, x_lo = _split(x)
        f_hi = fh_ref[lo_row:lo_row + seq, :]
        dot = functools.partial(jnp.dot, preferred_element_type=F32)
        return dot(f_hi, x_hi) + (dot(f_hi, x_lo) + dot(fl_ref[lo_row:lo_row + seq, :], x_hi))

    p_ref[0] = dft(0, even)
    q_ref[0] = jnp.where(row == 0, 0.0, dft(seq, odd))
    nyq_ref[0] = jnp.sum(jnp.where(row % 2 == 0, even, -even), axis=0, keepdims=True)


def _hyena_filters(seq, fwd_hi, fwd_lo, hy_w1, hy_b1, hy_w2, hy_b2, hy_freq, hy_w3):
    t = jnp.arange(seq, dtype=F32)
    bands = jnp.linspace(1e-4, HY_BANDS - 1, HY_BANDS, dtype=F32)
    ang = (2.0 * math.pi / seq) * t[:, None] * bands[None, :]
    z = jnp.concatenate([(t / seq)[:, None], jnp.cos(ang), -jnp.sin(ang),
                         jnp.zeros((seq, LANES - HY_EMB), F32)], axis=-1)
    w1 = jnp.pad(hy_w1.astype(F32), ((0, 0), (0, LANES - HY_EMB), (0, 0)))
    rate = jnp.linspace(HY_DECAY_MIN, HY_DECAY_MAX, HY_WIDTH, dtype=F32)
    rate = jnp.tile(rate, HY_ORDER).reshape(1, HY_ORDER * HY_WIDTH)
    width = HY_ORDER * HY_WIDTH
    n_cb = width // FILTER_COLS
    lay = lambda l, j: (l, 0, 0)
    out_spec = pl.BlockSpec((1, seq, FILTER_COLS), lambda l, j: (l, 0, j))
    return pl.pallas_call(
        functools.partial(_filter_kernel, seq),
        grid=(DEPTH, n_cb),
        in_specs=[
            pl.BlockSpec((seq, LANES), lambda l, j: (0, 0)),
            pl.BlockSpec((1, LANES, HY_HIDDEN), lay),
            pl.BlockSpec((1, 1, HY_HIDDEN), lay),
            pl.BlockSpec((1, HY_HIDDEN, HY_HIDDEN), lay),
            pl.BlockSpec((1, 1, HY_HIDDEN), lay),
            pl.BlockSpec((1, 2, HY_HIDDEN), lay),
            pl.BlockSpec((1, HY_HIDDEN, FILTER_COLS), lambda l, j: (l, 0, j)),
            pl.BlockSpec((1, HY_HIDDEN, FILTER_COLS), lambda l, j: (l, 0, n_cb + j)),
            pl.BlockSpec((1, FILTER_COLS), lambda l, j: (0, j)),
            _const_spec(fwd_hi.shape), _const_spec(fwd_lo.shape),
        ],
        out_specs=[out_spec, out_spec, pl.BlockSpec((1, 1, FILTER_COLS), lambda l, j: (l, 0, j))],
        out_shape=[jax.ShapeDtypeStruct((DEPTH, seq, width), F32), jax.ShapeDtypeStruct((DEPTH, seq, width), F32),
                   jax.ShapeDtypeStruct((DEPTH, 1, width), F32)],
        compiler_params=_cparams(2),
        name="hyena_filter_%d" % seq,
    )(z, w1, hy_b1.astype(F32).reshape(DEPTH, 1, HY_HIDDEN), hy_w2.astype(F32),
      hy_b2.astype(F32).reshape(DEPTH, 1, HY_HIDDEN), hy_freq.astype(F32), hy_w3.astype(F32), hy_w3.astype(F32),
      rate, fwd_hi, fwd_lo)


def _hyena_kernel(seq, x1_ref, x2_ref, v_ref, w1_ref, w2_ref, wv_ref, b1_ref, b2_ref, bv_ref,
                  f_ref, g_ref, p1_ref, p2_ref, q1_ref, q2_ref, n1_ref, n2_ref, bias_ref, o_ref):
    cn = o_ref.shape[1]
    row = lax.broadcasted_iota(jnp.int32, (seq, cn), 0)

    def short_conv(x_ref, w_ref, b_ref):
        x = x_ref[...].astype(F32)
        prev = jnp.where(row == 0, 0.0, pltpu.roll(x, 1, axis=0))
        nxt = jnp.where(row == seq - 1, 0.0, pltpu.roll(x, seq - 1, axis=0))
        return prev * w_ref[0:1, :] + x * w_ref[1:2, :] + nxt * w_ref[2:3, :] + b_ref[...]

    out = short_conv(v_ref, wv_ref, bv_ref)
    stages = ((x1_ref, w1_ref, b1_ref, p1_ref, q1_ref, n1_ref), (x2_ref, w2_ref, b2_ref, p2_ref, q2_ref, n2_ref))
    for o, (x_ref, w_ref, b_ref, p_ref, q_ref, n_ref) in enumerate(stages):
        z = _dot(f_ref[...], out)
        zr, zi = z[:seq], z[seq:]
        p, q = p_ref[0], q_ref[0]
        yr = zr * p - zi * q
        yi = jnp.where(row == 0, zi * n_ref[0], zr * q + zi * p)
        y = jnp.concatenate([yr.astype(BF16), yi.astype(BF16)], axis=0)
        conv = _dot(g_ref[...], y)
        out = short_conv(x_ref, w_ref, b_ref) * (conv + bias_ref[0, o:o + 1, :] * out)
    o_ref[...] = out.astype(o_ref.dtype)


def _hyena(path, layer, hy, conv_w, conv_b, fwd_hi, inv, filt, bias):
    seq = path.seq
    cn = 512 if seq <= 256 else 256
    nj = HY_WIDTH // cn
    p, q, nyq = filt
    rows = hy.shape[0]

    def col(part):
        return lambda j, b: (b, part * nj + j)

    def wcol(part):
        return lambda j, b: (layer, 0, part * nj + j)

    def fcol(o):
        return lambda j, b: (layer, 0, o * nj + j)

    spec_f = [pl.BlockSpec((1, seq, cn), fcol(o)) for o in range(HY_ORDER)]
    spec_n = [pl.BlockSpec((1, 1, cn), fcol(o)) for o in range(HY_ORDER)]
    return pl.pallas_call(
        functools.partial(_hyena_kernel, seq),
        grid=(nj, path.batch),
        in_specs=[pl.BlockSpec((seq, cn), col(part)) for part in range(3)]
        + [pl.BlockSpec((None, 3, cn), wcol(part)) for part in range(3)]
        + [pl.BlockSpec((None, 1, cn), wcol(part)) for part in range(3)]
        + [_const_spec(fwd_hi.shape), _const_spec(inv.shape)] + spec_f + spec_f + spec_n
        + [pl.BlockSpec((1, HY_ORDER, cn), lambda j, b: (layer, 0, j))],
        out_specs=pl.BlockSpec((seq, cn), lambda j, b: (b, j)),
        out_shape=jax.ShapeDtypeStruct((rows, HY_WIDTH), BF16),
        compiler_params=_cparams(2),
        name="hyena_lat" if path.latent else "hyena_ctx",
    )(hy, hy, hy, conv_w, conv_w, conv_w, conv_b, conv_b, conv_b, fwd_hi, inv, p, p, q, q, nyq, nyq, bias)


def _merge_kernel(path, x_ref, mod_ref, y_ref, u_ref, d_ref, ret_ref, hy_ref, gl_ref,
                  wglu_ref, wret_ref, why_ref, wout_ref, o_ref):
    r = path.row_in_step()
    m0 = _half_mask(S5_BLOCK_CH)
    ys = []
    for blk in range(S5_N_BLOCKS):
        y, u = _tm_rows(y_ref, blk, r)[...], _tm_rows(u_ref, blk, r)[...]
        if path.latent:
            y = jnp.where(m0, y, _tm_rows(y_ref, blk, r + 1)[...])
            u = u + _tm_rows(u_ref, blk, r + 1)[...]
        ys.append(y + d_ref[:, blk * S5_BLOCK_CH:(blk + 1) * S5_BLOCK_CH] * u)
    y_s5 = jnp.concatenate(ys, axis=1)
    ab = _dot(jax.nn.gelu(y_s5), wglu_ref[...])
    br_s5 = ab[:, :D_MODEL] * jax.nn.sigmoid(ab[:, D_MODEL:])
    br_ret = _dot(ret_ref[...], wret_ref[...])
    br_hy = _dot(hy_ref[...], why_ref[...])
    gates = jax.nn.sigmoid(gl_ref[...].astype(F32))
    merged = (gates[:, 0:D_MODEL] * br_s5 + gates[:, D_MODEL:2 * D_MODEL] * br_ret
              + gates[:, 2 * D_MODEL:] * br_hy)
    mix = _dot(merged, wout_ref[...])
    o_ref[...] = x_ref[...] + mod_ref[0, 2:3, :] * mix


def _merge(path, x, mod, y_tm, u_tm, d, ret, hyo, gl, w_glu, w_ret, w_hy, w_out):
    rows = x.shape[0]
    return pl.pallas_call(
        functools.partial(_merge_kernel, path),
        grid=(path.n_tiles,),
        in_specs=[
            pl.BlockSpec((TOKEN_TILE, D_MODEL), path.tok_index),
            pl.BlockSpec((1, 6, D_MODEL), path.mod_index),
            path.tm_spec(), path.tm_spec(),
            _const_spec((1, S5_WIDTH)),
            pl.BlockSpec((TOKEN_TILE, RET_WIDTH), path.tok_index),
            pl.BlockSpec((TOKEN_TILE, HY_WIDTH), path.tok_index),
            pl.BlockSpec((TOKEN_TILE, 3 * D_MODEL), path.tok_index),
            _const_spec(w_glu.shape), _const_spec(w_ret.shape), _const_spec(w_hy.shape), _const_spec(w_out.shape),
        ],
        out_specs=pl.BlockSpec((TOKEN_TILE, D_MODEL), path.tok_index),
        out_shape=jax.ShapeDtypeStruct((rows, D_MODEL), F32),
        compiler_params=_cparams(1),
        name="merge_lat" if path.latent else "merge_ctx",
    )(x, mod, y_tm, u_tm, d, ret, hyo, gl, w_glu, w_ret, w_hy, w_out)


def _ffn_kernel(final, x_ref, mod_ref, g_ref, win_ref, wout_ref, gf_ref, o_ref):
    x = x_ref[...]
    h = _rms_mod(x, g_ref[...], mod_ref[0, 4:5, :], mod_ref[0, 3:4, :]).astype(BF16)
    a = jnp.dot(h, win_ref[:, :D_FF], preferred_element_type=F32)
    b = jnp.dot(h, win_ref[:, D_FF:], preferred_element_type=F32)
    act = (a * jax.nn.sigmoid(a)) * b
    x = x + mod_ref[0, 5:6, :] * _dot(act, wout_ref[...])
    if final:
        x = (x * lax.rsqrt(jnp.mean(x * x, axis=-1, keepdims=True) + EPS)) * gf_ref[...]
    o_ref[...] = x


def _ffn(path, x, mod, norm_g, w_in, w_out, norm_f, final):
    rows = x.shape[0]
    tok = lambda i: (i, 0)
    nat_mod = lambda i: (1 + i // path.tiles_per_seq if path.latent else 0, 0, 0)
    return pl.pallas_call(
        functools.partial(_ffn_kernel, final),
        grid=(path.n_tiles,),
        in_specs=[
            pl.BlockSpec((TOKEN_TILE, D_MODEL), tok),
            pl.BlockSpec((1, 6, D_MODEL), nat_mod),
            _const_spec((1, D_MODEL)),
            _const_spec(w_in.shape), _const_spec(w_out.shape),
            _const_spec((1, D_MODEL)),
        ],
        out_specs=pl.BlockSpec((TOKEN_TILE, D_MODEL), tok),
        out_shape=jax.ShapeDtypeStruct((rows, D_MODEL), F32),
        compiler_params=_cparams(1),
        name="ffn_lat" if path.latent else "ffn_ctx",
    )(x, mod, norm_g, w_in, w_out, norm_f)


def kernel(x_prompt, x_sample, state_s5, state_ret, c, c_ctx, w_mod, b_mod, norm1, norm2, w_in, s5_lam_re, s5_lam_im, s5_log_dt, s5_b_re, s5_b_im, s5_c_re, s5_c_im, s5_d, w_s5_glu, ret_decay, w_ret_o, hy_conv_w, hy_conv_b, hy_w1, hy_b1, hy_w2, hy_b2, hy_freq, hy_w3, hy_bias, w_hy_o, w_out, w_ffn_in, w_ffn_out, norm_f):
    batch, seq, _ = x_prompt.shape
    dec_batch, dec_seq, _ = x_sample.shape
    ctx = _Path(batch, seq, latent=False)
    lat = _Path(dec_batch, dec_seq, latent=True)

    c_rows = jnp.concatenate([c_ctx[None, :], c, jnp.zeros((SUBLANES - 1 - dec_batch, D_MODEL), F32)], axis=0)
    mod_all = _modulation(c_rows.astype(F32), w_mod.astype(BF16), b_mod.astype(F32))
    mod_all = mod_all.reshape(DEPTH, SUBLANES, 6, D_MODEL)

    s5_par = dict(zip((ctx, lat), _s5_params(s5_lam_re, s5_lam_im, s5_log_dt, s5_b_re, s5_b_im, s5_c_re, s5_c_im)))
    tables, filters = {}, {}
    for path in (ctx, lat):
        fwd_hi, fwd_lo, inv = _dft_tables(path.seq)
        tables[path] = (fwd_hi, inv)
        filters[path] = _hyena_filters(path.seq, fwd_hi, fwd_lo, hy_w1, hy_b1, hy_w2, hy_b2, hy_freq, hy_w3)
    rope = _rope_tables(dec_seq)
    log_g_all = jnp.log1p(-jnp.exp(ret_decay.astype(F32)))
    norm_f2 = norm_f.astype(F32).reshape(1, D_MODEL)
    conv_w = hy_conv_w.astype(F32)
    conv_b = hy_conv_b.astype(F32).reshape(DEPTH, 1, 3 * HY_WIDTH)
    hy_bias = hy_bias.astype(F32)

    st = state_s5.astype(F32).reshape(dec_batch, DEPTH, 2, S5_N_BLOCKS, 2, lat.s5_cols, 2)
    h0_lat = jnp.transpose(st, (1, 2, 3, 6, 0, 4, 5)).reshape(DEPTH, 1, 2, S5_N_BLOCKS, 2, SUBLANES, lat.s5_cols)
    h0_ctx = jnp.zeros((ctx.s5_blocks, 2, S5_N_BLOCKS, 2, SUBLANES, ctx.s5_cols), F32)
    s0_ret = state_ret.astype(F32)

    w_in_b, w_glu_b, w_ret_b = w_in.astype(BF16), w_s5_glu.astype(BF16), w_ret_o.astype(BF16)
    w_hy_b, w_out_b = w_hy_o.astype(BF16), w_out.astype(BF16)
    w_fin_b, w_fout_b = w_ffn_in.astype(BF16), w_ffn_out.astype(BF16)

    xs = {ctx: x_prompt.astype(F32).reshape(batch * seq, D_MODEL),
          lat: x_sample.astype(F32).reshape(dec_batch * dec_seq, D_MODEL)}
    s5_states, ret_states = [], []
    for l in range(DEPTH):
        mod = mod_all[l]
        d_l = s5_d[l].astype(F32).reshape(1, S5_WIDTH)
        n1 = norm1[l].astype(F32).reshape(1, D_MODEL)
        n2 = norm2[l].astype(F32).reshape(1, D_MODEL)
        for path in (ctx, lat):
            x = xs[path]
            u_tm, qkvg, hy, gl = _in_proj(path, x, mod, n1, w_in_b[l])
            b_mat, c_mat, lam_t = s5_par[path]
            y_tm, h_fin = _s5(path, l, u_tm, b_mat, c_mat, lam_t, h0_lat[l] if path.latent else h0_ctx)
            if path.latent:
                ret, = _retention(path, qkvg, log_g_all[l], rope, s0_ret[:, l])
            else:
                ret, r_state = _retention(path, qkvg, log_g_all[l])
                ret_states.append(r_state)
                s5_states.append(h_fin)
            fwd_hi, inv = tables[path]
            hyo = _hyena(path, l, hy, conv_w, conv_b, fwd_hi, inv, filters[path], hy_bias)
            x = _merge(path, x, mod, y_tm, u_tm, d_l, ret, hyo, gl, w_glu_b[l], w_ret_b[l], w_hy_b[l], w_out_b[l])
            xs[path] = _ffn(path, x, mod, n2, w_fin_b[l], w_fout_b[l], norm_f2, final=(l == DEPTH - 1))

    y_prompt = xs[ctx].reshape(batch, seq, D_MODEL).astype(x_prompt.dtype)
    y_sample = xs[lat].reshape(dec_batch, dec_seq, D_MODEL).astype(x_sample.dtype)
    hf = jnp.stack(s5_states, axis=0).reshape(DEPTH, ctx.s5_blocks, 2, S5_N_BLOCKS, 2, SUBLANES, S5_BLOCK_GROUPS, S5_STATE)
    hf = jnp.transpose(hf, (1, 5, 0, 2, 3, 6, 7, 4)).reshape(batch, DEPTH, 2, S5_GROUPS, S5_STATE, 2)
    new_state_s5 = hf.astype(x_prompt.dtype)
    new_state_ret = jnp.stack(ret_states, axis=1).astype(x_prompt.dtype)
    return (y_prompt, y_sample, new_state_s5, new_state_ret)
```

```python
import functools
import math

import jax
import jax.numpy as jnp
import numpy as np
from jax import lax
from jax.experimental import pallas as pl
from jax.experimental.pallas import tpu as pltpu

F32 = jnp.float32
BF16 = jnp.bfloat16

D_MODEL = 1024
DEPTH = 2
GRID_W = 64
EPS = 1e-6
GN_EPS = 1e-5
S5_WIDTH = 512
S5_GROUP_CH = 16
S5_GROUPS = 32
S5_STATE = 64
RET_WIDTH = 512
RET_HEADS = 4
RET_DK = 128
ROPE_BASE = 10000.0
HY_WIDTH = 512
HY_ORDER = 2
HY_BANDS = 16
HY_EMB = 1 + 2 * HY_BANDS
HY_HIDDEN = 64
HY_DECAY_MIN = -math.log(1e-2) / 1.5
HY_DECAY_MAX = -math.log(1e-2) / 0.3
D_FF = 2816

LANES = 128
SUBLANES = 8
TOKEN_TILE = 256
S5_BLOCK_CH = LANES
S5_BLOCK_GROUPS = S5_BLOCK_CH // S5_GROUP_CH
S5_N_BLOCKS = S5_WIDTH // S5_BLOCK_CH
VMEM_LIMIT = 56 * 1024 * 1024


def _cparams(n_axes):
    return pltpu.CompilerParams(dimension_semantics=("arbitrary",) * n_axes, vmem_limit_bytes=VMEM_LIMIT)


def _const_spec(shape):
    nd = len(shape)
    return pl.BlockSpec(shape, lambda *_: (0,) * nd, pipeline_mode=pl.Buffered(1))


def _layer_spec(shape, layer):
    nd = len(shape) - 1
    return pl.BlockSpec((None,) + tuple(shape[1:]), lambda *_: (layer,) + (0,) * nd, pipeline_mode=pl.Buffered(1))


def _dot(a, b):
    return jnp.dot(a.astype(BF16), b.astype(BF16), preferred_element_type=F32)


def _split(a):
    hi = a.astype(BF16)
    return hi, (a - hi.astype(F32)).astype(BF16)


def _dot3(a, b):
    a_hi, a_lo = _split(a)
    b_hi, b_lo = _split(b)
    dot = functools.partial(jnp.dot, preferred_element_type=F32)
    return dot(a_hi, b_hi) + (dot(a_hi, b_lo) + dot(a_lo, b_hi))


def _rms_mod(x, g, scale, shift):
    y = x * lax.rsqrt(jnp.mean(x * x, axis=-1, keepdims=True) + EPS)
    return (y * g) * (1.0 + scale) + shift


def _mod_kernel(c_ref, w_ref, b_ref, o_ref):
    c = c_ref[...]
    cond = c * jax.nn.sigmoid(c)
    o_ref[0] = _dot(cond, w_ref[0]) + b_ref[0]


def _modulation(c_rows, w_mod, b_mod):
    n_chunk = 6
    return pl.pallas_call(
        _mod_kernel,
        grid=(DEPTH, n_chunk),
        in_specs=[
            pl.BlockSpec((SUBLANES, D_MODEL), lambda l, j: (0, 0)),
            pl.BlockSpec((1, D_MODEL, D_MODEL), lambda l, j: (l, 0, j)),
            pl.BlockSpec((1, 1, D_MODEL), lambda l, j: (l, 0, j)),
        ],
        out_specs=pl.BlockSpec((1, SUBLANES, D_MODEL), lambda l, j: (l, 0, j)),
        out_shape=jax.ShapeDtypeStruct((DEPTH, SUBLANES, 6 * D_MODEL), F32),
        compiler_params=_cparams(2),
        name="modulation",
    )(c_rows, w_mod, b_mod.reshape(DEPTH, 1, 6 * D_MODEL))


class _Path:
    def __init__(self, batch, seq, latent):
        self.batch = batch
        self.seq = seq
        self.latent = latent
        self.tiles_per_seq = seq // TOKEN_TILE
        self.n_tiles = batch * self.tiles_per_seq
        if latent:
            assert batch * 2 == SUBLANES
            self.s5_blocks = 1
            self.s5_cols = S5_BLOCK_GROUPS * S5_STATE // 2
            self.tiles_per_tm = batch
        else:
            assert batch % SUBLANES == 0 and seq == TOKEN_TILE
            self.s5_blocks = batch // SUBLANES
            self.s5_cols = S5_BLOCK_GROUPS * S5_STATE
            self.tiles_per_tm = SUBLANES
        self.s5_rows = seq * SUBLANES
        self.tm_rows = TOKEN_TILE * SUBLANES

    def tok(self, i):
        if self.latent:
            return (i % self.batch) * self.tiles_per_seq + i // self.batch
        return i

    def tok_index(self, i):
        return (self.tok(i), 0)

    def mod_spec(self, layer, natural=False):
        def index(i):
            seq_id = i // self.tiles_per_seq if natural else i % self.batch
            return (layer, 1 + seq_id if self.latent else 0, 0, 0)
        return pl.BlockSpec((None, 1, 6, D_MODEL), index)

    def tm_index(self, i):
        return (0, i // self.tiles_per_tm, 0)

    def tm_spec(self):
        return pl.BlockSpec((S5_N_BLOCKS, self.tm_rows, S5_BLOCK_CH), self.tm_index)

    def tm_shape(self):
        return jax.ShapeDtypeStruct((S5_N_BLOCKS, self.s5_blocks * self.s5_rows, S5_BLOCK_CH), F32)

    def row_in_step(self):
        r = pl.program_id(0) % self.tiles_per_tm
        return 2 * r if self.latent else r


def _half_mask(width):
    ch = lax.broadcasted_iota(jnp.int32, (1, width), 1)
    return (ch % S5_BLOCK_CH) < (S5_BLOCK_CH // 2)


def _tm_rows(ref, blk, r):
    return ref.at[blk, pl.ds(r, TOKEN_TILE, stride=SUBLANES), :]


_IN_SEGS = ((0, 512), (512, 2560), (2560, 4096), (4096, 7168))


def _in_proj_kernel(path, x_ref, mod_ref, g_ref, w_ref, u_ref, qkvg_ref, hy_ref, gl_ref):
    h = _rms_mod(x_ref[...], g_ref[...], mod_ref[0, 1:2, :], mod_ref[0, 0:1, :]).astype(BF16)
    u = jnp.dot(h, w_ref[:, 0:512], preferred_element_type=F32)
    r = path.row_in_step()
    m0 = _half_mask(S5_BLOCK_CH)
    for blk in range(S5_N_BLOCKS):
        ub = u[:, blk * S5_BLOCK_CH:(blk + 1) * S5_BLOCK_CH]
        if path.latent:
            _tm_rows(u_ref, blk, r)[...] = jnp.where(m0, ub, 0.0)
            _tm_rows(u_ref, blk, r + 1)[...] = jnp.where(m0, 0.0, ub)
        else:
            _tm_rows(u_ref, blk, r)[...] = ub
    for ref, (lo, hi) in zip((qkvg_ref, hy_ref, gl_ref), _IN_SEGS[1:]):
        ref[...] = jnp.dot(h, w_ref[:, lo:hi], preferred_element_type=F32).astype(ref.dtype)


def _in_proj(path, layer, x, mod, norm_g, w_in):
    rows = x.shape[0]
    return pl.pallas_call(
        functools.partial(_in_proj_kernel, path),
        grid=(path.n_tiles,),
        in_specs=[
            pl.BlockSpec((TOKEN_TILE, D_MODEL), path.tok_index),
            path.mod_spec(layer),
            _layer_spec(norm_g.shape, layer),
            _layer_spec(w_in.shape, layer),
        ],
        out_specs=[
            path.tm_spec(),
            pl.BlockSpec((TOKEN_TILE, 2048), path.tok_index),
            pl.BlockSpec((TOKEN_TILE, 1536), path.tok_index),
            pl.BlockSpec((TOKEN_TILE, 3072), path.tok_index),
        ],
        out_shape=[
            path.tm_shape(),
            jax.ShapeDtypeStruct((rows, 2048), BF16),
            jax.ShapeDtypeStruct((rows, 1536), BF16),
            jax.ShapeDtypeStruct((rows, 3072), BF16),
        ],
        compiler_params=_cparams(1),
        name="in_proj_lat" if path.latent else "in_proj_ctx",
    )(x, mod, norm_g, w_in)


S5_CHUNK = 512


def _s5_kernel(seq, cols, has_h0, u_ref, b_ref, c_ref, lam_ref, *rest):
    if has_h0:
        h0_ref, y_ref, hfin_ref, st_f, st_b = rest
    else:
        y_ref, hfin_ref, st_f, st_b = rest
    n_chunk = seq * SUBLANES // S5_CHUNK
    steps = S5_CHUNK // SUBLANES
    sts = (st_f, st_b)

    def chunk(ci):
        return pl.ds(pl.multiple_of(ci * S5_CHUNK, S5_CHUNK), S5_CHUNK)

    def expand(d, ci):
        sts[d][chunk(ci), :] = _dot(u_ref[chunk(ci), :], b_ref[d, 0])

    def contract(d, ci):
        y = _dot(sts[d][chunk(ci), :], c_ref[d, 0])
        if d == 0:
            y_ref[chunk(ci), :] = y
        else:
            y_ref[chunk(ci), :] += y

    def scan_chunk(d, ci, carry):
        hr, hi = carry
        lr, li = lam_ref[d, 0, 0], lam_ref[d, 0, 1]
        st = sts[d]
        for j in range(steps):
            i = ci * steps + j
            t = i if d == 0 else seq - 1 - i
            rows = pl.ds(pl.multiple_of(t * SUBLANES, SUBLANES), SUBLANES)
            nr = lr * hr - li * hi + st[rows, 0:cols]
            ni = lr * hi + li * hr + st[rows, cols:2 * cols]
            st[rows, 0:cols] = nr
            st[rows, cols:2 * cols] = ni
            hr, hi = nr, ni
        return hr, hi

    def init(d):
        if has_h0:
            return h0_ref[0, d, 0, 0], h0_ref[0, d, 0, 1]
        zero = jnp.zeros((SUBLANES, cols), F32)
        return zero, zero

    def phase_a(ci, carry):
        expand(0, ci)
        return carry

    def phase_b(ci, carry):
        expand(1, ci)
        return scan_chunk(0, ci, carry)

    def phase_c(ci, carry):
        contract(0, ci)
        return scan_chunk(1, ci, carry)

    def phase_d(ci, carry):
        contract(1, ci)
        return carry

    lax.fori_loop(0, n_chunk, phase_a, 0)
    hr, hi = lax.fori_loop(0, n_chunk, phase_b, init(0))
    hfin_ref[0, 0, 0, 0] = hr
    hfin_ref[0, 0, 0, 1] = hi
    hr, hi = lax.fori_loop(0, n_chunk, phase_c, init(1))
    hfin_ref[0, 1, 0, 0] = hr
    hfin_ref[0, 1, 0, 1] = hi
    lax.fori_loop(0, n_chunk, phase_d, 0)


def _s5(path, layer, u_tm, b_mat, c_mat, lam, h0=None):
    nb, rows, cols = path.s5_blocks, path.s5_rows, path.s5_cols
    st_shape = (nb, 2, S5_N_BLOCKS, 2, SUBLANES, cols)
    st_spec = pl.BlockSpec((1, 2, 1, 2, SUBLANES, cols), lambda b, g: (b, 0, g, 0, 0, 0))
    tm_spec = pl.BlockSpec((None, rows, S5_BLOCK_CH), lambda b, g: (g, b, 0))
    in_specs = [
        tm_spec,
        pl.BlockSpec((None, 2, 1, S5_BLOCK_CH, 2 * cols), lambda b, g: (layer, 0, g, 0, 0)),
        pl.BlockSpec((None, 2, 1, 2 * cols, S5_BLOCK_CH), lambda b, g: (layer, 0, g, 0, 0)),
        pl.BlockSpec((None, 2, 1, 2, SUBLANES, cols), lambda b, g: (layer, 0, g, 0, 0, 0)),
    ]
    args = [u_tm, b_mat, c_mat, lam]
    if h0 is not None:
        in_specs.append(pl.BlockSpec((None, 1, 2, 1, 2, SUBLANES, cols), lambda b, g: (layer, b, 0, g, 0, 0, 0)))
        args.append(h0)
    return pl.pallas_call(
        functools.partial(_s5_kernel, path.seq, cols, h0 is not None),
        grid=(nb, S5_N_BLOCKS),
        in_specs=in_specs,
        out_specs=[tm_spec, st_spec],
        out_shape=[path.tm_shape(), jax.ShapeDtypeStruct(st_shape, F32)],
        scratch_shapes=[pltpu.VMEM((rows, 2 * cols), F32), pltpu.VMEM((rows, 2 * cols), F32)],
        compiler_params=_cparams(2),
        name="s5_lat" if path.latent else "s5_ctx",
    )(*args)


def _s5_params(lam_re, lam_im, log_dt, b_re, b_im, c_re, c_im):
    lam = lax.complex(lam_re.astype(F32), lam_im.astype(F32))
    dt = jnp.exp(log_dt.astype(F32))[..., None]
    lam_bar = jnp.exp(lam * dt)
    b = lax.complex(b_re.astype(F32), b_im.astype(F32))
    b_bar = ((lam_bar - 1.0) / lam)[..., None] * b
    c = lax.complex(c_re.astype(F32), c_im.astype(F32))
    g8, nblk = S5_BLOCK_GROUPS, S5_N_BLOCKS
    cols = g8 * S5_STATE
    eye = np.eye(g8, dtype=np.float32)
    bb = b_bar.reshape(DEPTH, 2, nblk, g8, S5_STATE, S5_GROUP_CH)
    cc = c.reshape(DEPTH, 2, nblk, g8, S5_GROUP_CH, S5_STATE)

    def expand_b(part):
        return jnp.einsum('lrbgpc,gh->lrbgchp', part, eye).reshape(DEPTH, 2, nblk, S5_BLOCK_CH, cols)

    def expand_c(part):
        return jnp.einsum('lrbgcp,gh->lrbhpgc', part, eye).reshape(DEPTH, 2, nblk, cols, S5_BLOCK_CH)

    b_r, b_i = expand_b(jnp.real(bb)), expand_b(jnp.imag(bb))
    c_r, c_i = expand_c(jnp.real(cc)), -expand_c(jnp.imag(cc))
    half = cols // 2
    ctx = (jnp.concatenate([b_r, b_i], axis=-1).astype(BF16),
           jnp.concatenate([c_r, c_i], axis=-2).astype(BF16))
    lat = (jnp.concatenate([b_r[..., :half] + b_r[..., half:], b_i[..., :half] + b_i[..., half:]], axis=-1).astype(BF16),
           jnp.concatenate([c_r[..., :half, :] + c_r[..., half:, :], c_i[..., :half, :] + c_i[..., half:, :]],
                           axis=-2).astype(BF16))
    lb = jnp.stack([jnp.real(lam_bar), jnp.imag(lam_bar)], axis=2)
    lam_ctx = jnp.broadcast_to(lb.reshape(DEPTH, 2, 2, nblk, 1, cols), (DEPTH, 2, 2, nblk, SUBLANES, cols))
    lam_lat = jnp.broadcast_to(lb.reshape(DEPTH, 2, 2, nblk, 1, 2, half), (DEPTH, 2, 2, nblk, SUBLANES // 2, 2, half))
    lam_ctx = jnp.transpose(lam_ctx, (0, 1, 3, 2, 4, 5))
    lam_lat = jnp.transpose(lam_lat.reshape(DEPTH, 2, 2, nblk, SUBLANES, half), (0, 1, 3, 2, 4, 5))
    return ctx + (lam_ctx,), lat + (lam_lat,)


def _ret_kernel(seq, latent, n_q, layer, lg_ref, q_ref, k_ref, v_ref, g_ref, *rest):
    if latent:
        cq_ref, sq_ref, ck_ref, sk_ref, s0_ref, o_ref = rest
    else:
        o_ref, st_ref = rest
    tq = TOKEN_TILE
    q0 = (pl.program_id(0) % n_q) * tq
    scale = RET_DK ** -0.5
    tpos = q0 + lax.broadcasted_iota(jnp.int32, (tq, seq), 0)
    spos = lax.broadcasted_iota(jnp.int32, (tq, seq), 1)
    diff = (tpos - spos).astype(F32)
    trow = (q0 + lax.broadcasted_iota(jnp.int32, (tq, RET_DK), 0)).astype(F32)
    srow = lax.broadcasted_iota(jnp.int32, (seq, RET_DK), 0).astype(F32)
    for h in range(RET_HEADS):
        sl = slice(h * RET_DK, (h + 1) * RET_DK)
        lgf = lg_ref[layer, 0, h]
        lgb = lg_ref[layer, 1, h]
        q = q_ref[:, sl].astype(F32)
        k = k_ref[:, sl].astype(F32)
        v = v_ref[:, sl]
        if latent:
            qr = q * cq_ref[...] + pltpu.roll(q, RET_DK // 2, axis=1) * sq_ref[...]
            kr = k * ck_ref[...] + pltpu.roll(k, RET_DK // 2, axis=1) * sk_ref[...]
        else:
            qr, kr = q, k
        scores = lax.dot_general(qr.astype(BF16), kr.astype(BF16), (((1,), (1,)), ((), ())),
                                 preferred_element_type=F32)
        decay = jnp.exp(jnp.where(diff >= 0, lgf * diff, -lgb * diff)) * scale
        out = _dot(scores * decay, v)
        if latent:
            out = out + _dot(q, s0_ref[0, 0, h]) * jnp.exp(lgf * (trow + 1.0))
            out = out + _dot(q, s0_ref[0, 1, h]) * jnp.exp(lgb * (seq - 1.0 - trow))
        else:
            kf = (k * (jnp.exp(lgf * (seq - 1.0 - srow)) * scale)).astype(BF16)
            kb = (k * (jnp.exp(lgb * srow) * scale)).astype(BF16)
            tn = (((0,), (0,)), ((), ()))
            st_ref[0, 0, h] = lax.dot_general(kf, v, tn, preferred_element_type=F32)
            st_ref[0, 1, h] = lax.dot_general(kb, v, tn, preferred_element_type=F32)
        xc = out - jnp.mean(out, axis=-1, keepdims=True)
        nrm = xc * lax.rsqrt(jnp.mean(xc * xc, axis=-1, keepdims=True) + GN_EPS)
        g = g_ref[:, sl].astype(F32)
        o_ref[:, sl] = (nrm * (g * jax.nn.sigmoid(g))).astype(o_ref.dtype)


def _rope_tables(seq):
    n_rows = seq // GRID_W
    rows = jnp.repeat(jnp.arange(n_rows, dtype=F32), GRID_W)
    cols = jnp.tile(jnp.arange(GRID_W, dtype=F32), n_rows)
    half = RET_DK // 2
    n_freq = half // 2
    inv = ROPE_BASE ** (-jnp.arange(n_freq, dtype=F32) / n_freq)
    ang = jnp.concatenate([rows[:, None] * inv, cols[:, None] * inv], axis=-1)
    cos = jnp.cos(ang)
    sin = jnp.sin(ang)
    return jnp.concatenate([cos, cos], axis=-1), jnp.concatenate([-sin, sin], axis=-1)


def _retention(path, layer, qkvg, log_g, rope=None, s0=None):
    seq, n_q = path.seq, path.tiles_per_seq
    rows = qkvg.shape[0]
    in_specs = [
        pl.BlockSpec(memory_space=pltpu.SMEM),
        pl.BlockSpec((TOKEN_TILE, RET_WIDTH), lambda i: (i, 0)),
        pl.BlockSpec((seq, RET_WIDTH), lambda i: (i // n_q, 1)),
        pl.BlockSpec((seq, RET_WIDTH), lambda i: (i // n_q, 2)),
        pl.BlockSpec((TOKEN_TILE, RET_WIDTH), lambda i: (i, 3)),
    ]
    args = [log_g, qkvg, qkvg, qkvg, qkvg]
    out_specs = [pl.BlockSpec((TOKEN_TILE, RET_WIDTH), lambda i: (i, 0))]
    out_shape = [jax.ShapeDtypeStruct((rows, RET_WIDTH), BF16)]
    if path.latent:
        cos, sin = rope
        in_specs += [
            pl.BlockSpec((TOKEN_TILE, RET_DK), lambda i: (i % n_q, 0)),
            pl.BlockSpec((TOKEN_TILE, RET_DK), lambda i: (i % n_q, 0)),
            _const_spec((seq, RET_DK)),
            _const_spec((seq, RET_DK)),
            pl.BlockSpec((1, None, 2, RET_HEADS, RET_DK, RET_DK), lambda i: (i // n_q, layer, 0, 0, 0, 0)),
        ]
        args += [cos, sin, cos, sin, s0]
    else:
        out_specs.append(pl.BlockSpec((1, 2, RET_HEADS, RET_DK, RET_DK), lambda i: (i, 0, 0, 0, 0)))
        out_shape.append(jax.ShapeDtypeStruct((path.batch, 2, RET_HEADS, RET_DK, RET_DK), F32))
    return pl.pallas_call(
        functools.partial(_ret_kernel, seq, path.latent, n_q, layer),
        grid=(path.n_tiles,),
        in_specs=in_specs,
        out_specs=out_specs,
        out_shape=out_shape,
        compiler_params=_cparams(1),
        name="retention_lat" if path.latent else "retention_ctx",
    )(*args)


FILTER_COLS = 512
DFT_SPLIT = 32


def _dft_tables(seq):
    n = 2 * seq
    theta = 2.0 * math.pi / n

    def cos_sin(cols, stride):
        k = lax.broadcasted_iota(jnp.int32, (seq, cols), 0)
        j = lax.broadcasted_iota(jnp.int32, (seq, cols), 1)
        ang = ((k * (j * stride)) % n).astype(F32) * theta
        return jnp.cos(ang), jnp.sin(ang)

    (c1, s1), (c2, s2) = cos_sin(seq // DFT_SPLIT, DFT_SPLIT), cos_sin(DFT_SPLIT, 1)
    cos = (c1[:, :, None] * c2[:, None, :] - s1[:, :, None] * s2[:, None, :]).reshape(seq, seq)
    sin = (s1[:, :, None] * c2[:, None, :] + c1[:, :, None] * s2[:, None, :]).reshape(seq, seq)
    k = lax.broadcasted_iota(jnp.int32, (seq, seq), 0)
    t = lax.broadcasted_iota(jnp.int32, (seq, seq), 1)
    alt = jnp.where(t % 2 == 0, 1.0, -1.0)
    fwd = jnp.concatenate([cos, jnp.where(k == 0, alt, -sin)], axis=0)
    wk = jnp.where(k == 0, 1.0, 2.0) / n
    inv_c = (wk * cos).T
    inv_s = jnp.where(k == 0, alt / n, -(2.0 / n) * sin).T
    inv = jnp.concatenate([inv_c, inv_s], axis=1)
    return fwd.astype(BF16), inv.astype(BF16)


def _filter_kernel(seq, z_ref, w1_ref, b1_ref, w2_ref, b2_ref, fr_ref, w3f_ref, w3b_ref, rate_ref,
                   f_ref, p_ref, q_ref, nyq_ref):
    cb = p_ref.shape[-1]
    hid = jnp.sin(fr_ref[0, 0:1, :] * (_dot3(z_ref[...], w1_ref[0]) + b1_ref[0]))
    hid = jnp.sin(fr_ref[0, 1:2, :] * (_dot3(hid, w2_ref[0]) + b2_ref[0]))
    row = lax.broadcasted_iota(jnp.int32, (seq, cb), 0)
    win = jnp.exp(-(row.astype(F32) / seq) * rate_ref[...])
    fwd = _dot3(hid, w3f_ref[0]) * win
    bwd = jnp.where(row == 0, 0.0, _dot3(hid, w3b_ref[0]) * win)
    scale = lax.rsqrt(jnp.sum(fwd * fwd + bwd * bwd, axis=0, keepdims=True) + EPS)
    even = (fwd + bwd) * scale
    odd = (fwd - bwd) * scale
    p_ref[0] = _dot(f_ref[0:seq, :], even)
    q_ref[0] = jnp.where(row == 0, 0.0, _dot(f_ref[seq:2 * seq, :], odd))
    nyq_ref[0] = jnp.sum(jnp.where(row % 2 == 0, even, -even), axis=0, keepdims=True)


def _hyena_filters(seq, fwd, hy_w1, hy_b1, hy_w2, hy_b2, hy_freq, hy_w3):
    t = jnp.arange(seq, dtype=F32)
    bands = jnp.linspace(1e-4, HY_BANDS - 1, HY_BANDS, dtype=F32)
    ang = (2.0 * math.pi / seq) * t[:, None] * bands[None, :]
    z = jnp.concatenate([(t / seq)[:, None], jnp.cos(ang), -jnp.sin(ang),
                         jnp.zeros((seq, LANES - HY_EMB), F32)], axis=-1)
    w1 = jnp.pad(hy_w1.astype(F32), ((0, 0), (0, LANES - HY_EMB), (0, 0)))
    rate = jnp.linspace(HY_DECAY_MIN, HY_DECAY_MAX, HY_WIDTH, dtype=F32)
    rate = jnp.tile(rate, HY_ORDER).reshape(1, HY_ORDER * HY_WIDTH)
    width = HY_ORDER * HY_WIDTH
    n_cb = width // FILTER_COLS
    lay = lambda l, j: (l, 0, 0)
    out_spec = pl.BlockSpec((1, seq, FILTER_COLS), lambda l, j: (l, 0, j))
    return pl.pallas_call(
        functools.partial(_filter_kernel, seq),
        grid=(DEPTH, n_cb),
        in_specs=[
            pl.BlockSpec((seq, LANES), lambda l, j: (0, 0)),
            pl.BlockSpec((1, LANES, HY_HIDDEN), lay),
            pl.BlockSpec((1, 1, HY_HIDDEN), lay),
            pl.BlockSpec((1, HY_HIDDEN, HY_HIDDEN), lay),
            pl.BlockSpec((1, 1, HY_HIDDEN), lay),
            pl.BlockSpec((1, 2, HY_HIDDEN), lay),
            pl.BlockSpec((1, HY_HIDDEN, FILTER_COLS), lambda l, j: (l, 0, j)),
            pl.BlockSpec((1, HY_HIDDEN, FILTER_COLS), lambda l, j: (l, 0, n_cb + j)),
            pl.BlockSpec((1, FILTER_COLS), lambda l, j: (0, j)),
            _const_spec(fwd.shape),
        ],
        out_specs=[out_spec, out_spec, pl.BlockSpec((1, 1, FILTER_COLS), lambda l, j: (l, 0, j))],
        out_shape=[jax.ShapeDtypeStruct((DEPTH, seq, width), F32), jax.ShapeDtypeStruct((DEPTH, seq, width), F32),
                   jax.ShapeDtypeStruct((DEPTH, 1, width), F32)],
        compiler_params=_cparams(2),
        name="hyena_filter_%d" % seq,
    )(z, w1, hy_b1.astype(F32).reshape(DEPTH, 1, HY_HIDDEN), hy_w2.astype(F32),
      hy_b2.astype(F32).reshape(DEPTH, 1, HY_HIDDEN), hy_freq.astype(F32), hy_w3.astype(F32), hy_w3.astype(F32),
      rate, fwd)


def _hyena_kernel(seq, x1_ref, x2_ref, v_ref, w1_ref, w2_ref, wv_ref, b1_ref, b2_ref, bv_ref,
                  f_ref, g_ref, p1_ref, p2_ref, q1_ref, q2_ref, n1_ref, n2_ref, bias_ref, o_ref):
    cn = o_ref.shape[1]
    row = lax.broadcasted_iota(jnp.int32, (seq, cn), 0)

    def short_conv(x_ref, w_ref, b_ref):
        x = x_ref[...].astype(F32)
        prev = jnp.where(row == 0, 0.0, pltpu.roll(x, 1, axis=0))
        nxt = jnp.where(row == seq - 1, 0.0, pltpu.roll(x, seq - 1, axis=0))
        return prev * w_ref[0:1, :] + x * w_ref[1:2, :] + nxt * w_ref[2:3, :] + b_ref[...]

    out = short_conv(v_ref, wv_ref, bv_ref)
    stages = ((x1_ref, w1_ref, b1_ref, p1_ref, q1_ref, n1_ref), (x2_ref, w2_ref, b2_ref, p2_ref, q2_ref, n2_ref))
    for o, (x_ref, w_ref, b_ref, p_ref, q_ref, n_ref) in enumerate(stages):
        z = _dot(f_ref[...], out)
        zr, zi = z[:seq], z[seq:]
        p, q = p_ref[0], q_ref[0]
        yr = zr * p - zi * q
        yi = jnp.where(row == 0, zi * n_ref[0], zr * q + zi * p)
        y = jnp.concatenate([yr.astype(BF16), yi.astype(BF16)], axis=0)
        conv = _dot(g_ref[...], y)
        out = short_conv(x_ref, w_ref, b_ref) * (conv + bias_ref[0, o:o + 1, :] * out)
    o_ref[...] = out.astype(o_ref.dtype)


def _hyena(path, layer, hy, conv_w, conv_b, fwd_hi, inv, filt, bias):
    seq = path.seq
    cn = 512 if seq <= 256 else 256
    nj = HY_WIDTH // cn
    p, q, nyq = filt
    rows = hy.shape[0]

    def col(part):
        return lambda j, b: (b, part * nj + j)

    def wcol(part):
        return lambda j, b: (layer, 0, part * nj + j)

    def fcol(o):
        return lambda j, b: (layer, 0, o * nj + j)

    spec_f = [pl.BlockSpec((1, seq, cn), fcol(o)) for o in range(HY_ORDER)]
    spec_n = [pl.BlockSpec((1, 1, cn), fcol(o)) for o in range(HY_ORDER)]
    return pl.pallas_call(
        functools.partial(_hyena_kernel, seq),
        grid=(nj, path.batch),
        in_specs=[pl.BlockSpec((seq, cn), col(part)) for part in range(3)]
        + [pl.BlockSpec((None, 3, cn), wcol(part)) for part in range(3)]
        + [pl.BlockSpec((None, 1, cn), wcol(part)) for part in range(3)]
        + [_const_spec(fwd_hi.shape), _const_spec(inv.shape)] + spec_f + spec_f + spec_n
        + [pl.BlockSpec((1, HY_ORDER, cn), lambda j, b: (layer, 0, j))],
        out_specs=pl.BlockSpec((seq, cn), lambda j, b: (b, j)),
        out_shape=jax.ShapeDtypeStruct((rows, HY_WIDTH), BF16),
        compiler_params=_cparams(2),
        name="hyena_lat" if path.latent else "hyena_ctx",
    )(hy, hy, hy, conv_w, conv_w, conv_w, conv_b, conv_b, conv_b, fwd_hi, inv, p, p, q, q, nyq, nyq, bias)


def _merge_kernel(path, x_ref, mod_ref, y_ref, u_ref, d_ref, ret_ref, hy_ref, gl_ref,
                  wglu_ref, wret_ref, why_ref, wout_ref, o_ref):
    r = path.row_in_step()
    m0 = _half_mask(S5_BLOCK_CH)
    ys = []
    for blk in range(S5_N_BLOCKS):
        y, u = _tm_rows(y_ref, blk, r)[...], _tm_rows(u_ref, blk, r)[...]
        if path.latent:
            y = jnp.where(m0, y, _tm_rows(y_ref, blk, r + 1)[...])
            u = u + _tm_rows(u_ref, blk, r + 1)[...]
        ys.append(y + d_ref[:, blk * S5_BLOCK_CH:(blk + 1) * S5_BLOCK_CH] * u)
    y_s5 = jnp.concatenate(ys, axis=1)
    ab = _dot(jax.nn.gelu(y_s5), wglu_ref[...])
    br_s5 = ab[:, :D_MODEL] * jax.nn.sigmoid(ab[:, D_MODEL:])
    br_ret = _dot(ret_ref[...], wret_ref[...])
    br_hy = _dot(hy_ref[...], why_ref[...])
    gates = jax.nn.sigmoid(gl_ref[...].astype(F32))
    merged = (gates[:, 0:D_MODEL] * br_s5 + gates[:, D_MODEL:2 * D_MODEL] * br_ret
              + gates[:, 2 * D_MODEL:] * br_hy)
    mix = _dot(merged, wout_ref[...])
    o_ref[...] = x_ref[...] + mod_ref[0, 2:3, :] * mix


def _merge(path, layer, x, mod, y_tm, u_tm, d, ret, hyo, gl, w_glu, w_ret, w_hy, w_out):
    rows = x.shape[0]
    return pl.pallas_call(
        functools.partial(_merge_kernel, path),
        grid=(path.n_tiles,),
        in_specs=[
            pl.BlockSpec((TOKEN_TILE, D_MODEL), path.tok_index),
            path.mod_spec(layer),
            path.tm_spec(), path.tm_spec(),
            _layer_spec(d.shape, layer),
            pl.BlockSpec((TOKEN_TILE, RET_WIDTH), path.tok_index),
            pl.BlockSpec((TOKEN_TILE, HY_WIDTH), path.tok_index),
            pl.BlockSpec((TOKEN_TILE, 3 * D_MODEL), path.tok_index),
            _layer_spec(w_glu.shape, layer), _layer_spec(w_ret.shape, layer),
            _layer_spec(w_hy.shape, layer), _layer_spec(w_out.shape, layer),
        ],
        out_specs=pl.BlockSpec((TOKEN_TILE, D_MODEL), path.tok_index),
        out_shape=jax.ShapeDtypeStruct((rows, D_MODEL), F32),
        compiler_params=_cparams(1),
        name="merge_lat" if path.latent else "merge_ctx",
    )(x, mod, y_tm, u_tm, d, ret, hyo, gl, w_glu, w_ret, w_hy, w_out)


def _ffn_kernel(final, x_ref, mod_ref, g_ref, win_ref, wout_ref, gf_ref, o_ref):
    x = x_ref[...]
    h = _rms_mod(x, g_ref[...], mod_ref[0, 4:5, :], mod_ref[0, 3:4, :]).astype(BF16)
    a = jnp.dot(h, win_ref[:, :D_FF], preferred_element_type=F32)
    b = jnp.dot(h, win_ref[:, D_FF:], preferred_element_type=F32)
    act = (a * jax.nn.sigmoid(a)) * b
    x = x + mod_ref[0, 5:6, :] * _dot(act, wout_ref[...])
    if final:
        x = (x * lax.rsqrt(jnp.mean(x * x, axis=-1, keepdims=True) + EPS)) * gf_ref[...]
    o_ref[...] = x


def _ffn(path, layer, x, mod, norm_g, w_in, w_out, norm_f, final):
    rows = x.shape[0]
    tok = lambda i: (i, 0)
    return pl.pallas_call(
        functools.partial(_ffn_kernel, final),
        grid=(path.n_tiles,),
        in_specs=[
            pl.BlockSpec((TOKEN_TILE, D_MODEL), tok),
            path.mod_spec(layer, natural=True),
            _layer_spec(norm_g.shape, layer),
            _layer_spec(w_in.shape, layer), _layer_spec(w_out.shape, layer),
            _const_spec((1, D_MODEL)),
        ],
        out_specs=pl.BlockSpec((TOKEN_TILE, D_MODEL), tok),
        out_shape=jax.ShapeDtypeStruct((rows, D_MODEL), F32),
        compiler_params=_cparams(1),
        name="ffn_lat" if path.latent else "ffn_ctx",
    )(x, mod, norm_g, w_in, w_out, norm_f)


def kernel(x_prompt, x_sample, state_s5, state_ret, c, c_ctx, w_mod, b_mod, norm1, norm2, w_in, s5_lam_re, s5_lam_im, s5_log_dt, s5_b_re, s5_b_im, s5_c_re, s5_c_im, s5_d, w_s5_glu, ret_decay, w_ret_o, hy_conv_w, hy_conv_b, hy_w1, hy_b1, hy_w2, hy_b2, hy_freq, hy_w3, hy_bias, w_hy_o, w_out, w_ffn_in, w_ffn_out, norm_f):
    batch, seq, _ = x_prompt.shape
    dec_batch, dec_seq, _ = x_sample.shape
    ctx = _Path(batch, seq, latent=False)
    lat = _Path(dec_batch, dec_seq, latent=True)

    c_rows = jnp.concatenate([c_ctx[None, :], c, jnp.zeros((SUBLANES - 1 - dec_batch, D_MODEL), F32)], axis=0)
    mod_all = _modulation(c_rows.astype(F32), w_mod.astype(BF16), b_mod.astype(F32))
    mod_all = mod_all.reshape(DEPTH, SUBLANES, 6, D_MODEL)

    s5_par = dict(zip((ctx, lat), _s5_params(s5_lam_re, s5_lam_im, s5_log_dt, s5_b_re, s5_b_im, s5_c_re, s5_c_im)))
    tables, filters = {}, {}
    for path in (ctx, lat):
        tables[path] = _dft_tables(path.seq)
        filters[path] = _hyena_filters(path.seq, tables[path][0], hy_w1, hy_b1, hy_w2, hy_b2, hy_freq, hy_w3)
    rope = _rope_tables(dec_seq)
    log_g = jnp.log1p(-jnp.exp(ret_decay.astype(F32)))
    norm_f2 = norm_f.astype(F32).reshape(1, D_MODEL)
    conv_w = hy_conv_w.astype(F32)
    conv_b = hy_conv_b.astype(F32).reshape(DEPTH, 1, 3 * HY_WIDTH)
    hy_bias = hy_bias.astype(F32)
    d_skip = s5_d.astype(F32).reshape(DEPTH, 1, S5_WIDTH)
    n1 = norm1.astype(F32).reshape(DEPTH, 1, D_MODEL)
    n2 = norm2.astype(F32).reshape(DEPTH, 1, D_MODEL)

    st = state_s5.astype(F32).reshape(dec_batch, DEPTH, 2, S5_N_BLOCKS, 2, lat.s5_cols, 2)
    h0_lat = jnp.transpose(st, (1, 2, 3, 6, 0, 4, 5)).reshape(DEPTH, 1, 2, S5_N_BLOCKS, 2, SUBLANES, lat.s5_cols)
    s0_ret = state_ret.astype(F32)

    w_in_b, w_glu_b, w_ret_b = w_in.astype(BF16), w_s5_glu.astype(BF16), w_ret_o.astype(BF16)
    w_hy_b, w_out_b = w_hy_o.astype(BF16), w_out.astype(BF16)
    w_fin_b, w_fout_b = w_ffn_in.astype(BF16), w_ffn_out.astype(BF16)

    xs = {ctx: x_prompt.astype(F32).reshape(batch * seq, D_MODEL),
          lat: x_sample.astype(F32).reshape(dec_batch * dec_seq, D_MODEL)}
    s5_states, ret_states = [], []
    for l in range(DEPTH):
        for path in (ctx, lat):
            x = xs[path]
            u_tm, qkvg, hy, gl = _in_proj(path, l, x, mod_all, n1, w_in_b)
            b_mat, c_mat, lam_t = s5_par[path]
            y_tm, h_fin = _s5(path, l, u_tm, b_mat, c_mat, lam_t, h0_lat if path.latent else None)
            if path.latent:
                ret, = _retention(path, l, qkvg, log_g, rope, s0_ret)
            else:
                ret, r_state = _retention(path, l, qkvg, log_g)
                ret_states.append(r_state)
                s5_states.append(h_fin)
            fwd, inv = tables[path]
            hyo = _hyena(path, l, hy, conv_w, conv_b, fwd, inv, filters[path], hy_bias)
            x = _merge(path, l, x, mod_all, y_tm, u_tm, d_skip, ret, hyo, gl, w_glu_b, w_ret_b, w_hy_b, w_out_b)
            xs[path] = _ffn(path, l, x, mod_all, n2, w_fin_b, w_fout_b, norm_f2, final=(l == DEPTH - 1))

    y_prompt = xs[ctx].reshape(batch, seq, D_MODEL).astype(x_prompt.dtype)
    y_sample = xs[lat].reshape(dec_batch, dec_seq, D_MODEL).astype(x_sample.dtype)
    hf = jnp.stack(s5_states, axis=0).reshape(DEPTH, ctx.s5_blocks, 2, S5_N_BLOCKS, 2, SUBLANES, S5_BLOCK_GROUPS, S5_STATE)
    hf = jnp.transpose(hf, (1, 5, 0, 2, 3, 6, 7, 4)).reshape(batch, DEPTH, 2, S5_GROUPS, S5_STATE, 2)
    new_state_s5 = hf.astype(x_prompt.dtype)
    new_state_ret = jnp.stack(ret_states, axis=1).astype(x_prompt.dtype)
    return (y_prompt, y_sample, new_state_s5, new_state_ret)
```

```python
import functools
import math

import jax
import jax.numpy as jnp
import numpy as np
from jax import lax
from jax.experimental import pallas as pl
from jax.experimental.pallas import tpu as pltpu

F32 = jnp.float32
BF16 = jnp.bfloat16

D_MODEL = 1024
DEPTH = 2
GRID_W = 64
EPS = 1e-6
GN_EPS = 1e-5
S5_WIDTH = 512
S5_GROUP_CH = 16
S5_GROUPS = 32
S5_STATE = 64
RET_WIDTH = 512
RET_HEADS = 4
RET_DK = 128
ROPE_BASE = 10000.0
HY_WIDTH = 512
HY_ORDER = 2
HY_BANDS = 16
HY_EMB = 1 + 2 * HY_BANDS
HY_HIDDEN = 64
HY_DECAY_MIN = -math.log(1e-2) / 1.5
HY_DECAY_MAX = -math.log(1e-2) / 0.3
D_FF = 2816

LANES = 128
SUBLANES = 8
TOKEN_TILE = 256
S5_BLOCK_CH = LANES
S5_BLOCK_GROUPS = S5_BLOCK_CH // S5_GROUP_CH
S5_N_BLOCKS = S5_WIDTH // S5_BLOCK_CH
VMEM_LIMIT = 56 * 1024 * 1024


def _cparams(n_axes):
    return pltpu.CompilerParams(dimension_semantics=("arbitrary",) * n_axes, vmem_limit_bytes=VMEM_LIMIT)


def _const_spec(shape):
    nd = len(shape)
    return pl.BlockSpec(shape, lambda *_: (0,) * nd, pipeline_mode=pl.Buffered(1))


def _layer_spec(shape, layer):
    nd = len(shape) - 1
    return pl.BlockSpec((None,) + tuple(shape[1:]), lambda *_: (layer,) + (0,) * nd, pipeline_mode=pl.Buffered(1))


def _dot(a, b):
    return jnp.dot(a.astype(BF16), b.astype(BF16), preferred_element_type=F32)


def _split(a):
    hi = a.astype(BF16)
    return hi, (a - hi.astype(F32)).astype(BF16)


def _dot3(a, b):
    a_hi, a_lo = _split(a)
    b_hi, b_lo = _split(b)
    dot = functools.partial(jnp.dot, preferred_element_type=F32)
    return dot(a_hi, b_hi) + (dot(a_hi, b_lo) + dot(a_lo, b_hi))


def _sigmoid(x):
    return 0.5 * jnp.tanh(0.5 * x) + 0.5


def _rms_mod(x, g, scale, shift):
    y = x * lax.rsqrt(jnp.mean(x * x, axis=-1, keepdims=True) + EPS)
    return (y * g) * (1.0 + scale) + shift


def _mod_kernel(c_ref, w_ref, b_ref, o_ref):
    c = c_ref[...]
    cond = c * jax.nn.sigmoid(c)
    o_ref[0] = _dot(cond, w_ref[0]) + b_ref[0]


def _modulation(c_rows, w_mod, b_mod):
    n_chunk = 6
    return pl.pallas_call(
        _mod_kernel,
        grid=(DEPTH, n_chunk),
        in_specs=[
            pl.BlockSpec((SUBLANES, D_MODEL), lambda l, j: (0, 0)),
            pl.BlockSpec((1, D_MODEL, D_MODEL), lambda l, j: (l, 0, j)),
            pl.BlockSpec((1, 1, D_MODEL), lambda l, j: (l, 0, j)),
        ],
        out_specs=pl.BlockSpec((1, SUBLANES, D_MODEL), lambda l, j: (l, 0, j)),
        out_shape=jax.ShapeDtypeStruct((DEPTH, SUBLANES, 6 * D_MODEL), F32),
        compiler_params=_cparams(2),
        name="modulation",
    )(c_rows, w_mod, b_mod.reshape(DEPTH, 1, 6 * D_MODEL))


class _Path:
    def __init__(self, batch, seq, latent):
        self.batch = batch
        self.seq = seq
        self.latent = latent
        self.tiles_per_seq = seq // TOKEN_TILE
        self.n_tiles = batch * self.tiles_per_seq
        if latent:
            assert batch * 2 == SUBLANES
            self.s5_blocks = 1
            self.s5_cols = S5_BLOCK_GROUPS * S5_STATE // 2
            self.tiles_per_tm = batch
        else:
            assert batch % SUBLANES == 0 and seq == TOKEN_TILE
            self.s5_blocks = batch // SUBLANES
            self.s5_cols = S5_BLOCK_GROUPS * S5_STATE
            self.tiles_per_tm = SUBLANES
        self.s5_rows = seq * SUBLANES
        self.tm_rows = TOKEN_TILE * SUBLANES

    def tok(self, i):
        if self.latent:
            return (i % self.batch) * self.tiles_per_seq + i // self.batch
        return i

    def tok_index(self, i):
        return (self.tok(i), 0)

    def mod_spec(self, layer, natural=False):
        def index(i):
            seq_id = i // self.tiles_per_seq if natural else i % self.batch
            return (layer, 1 + seq_id if self.latent else 0, 0, 0)
        return pl.BlockSpec((None, 1, 6, D_MODEL), index)

    def tm_index(self, i):
        return (0, i // self.tiles_per_tm, 0)

    def tm_spec(self):
        return pl.BlockSpec((S5_N_BLOCKS, self.tm_rows, S5_BLOCK_CH), self.tm_index)

    def tm_shape(self):
        return jax.ShapeDtypeStruct((S5_N_BLOCKS, self.s5_blocks * self.s5_rows, S5_BLOCK_CH), F32)

    def row_in_step(self):
        r = pl.program_id(0) % self.tiles_per_tm
        return 2 * r if self.latent else r


def _half_mask(width):
    ch = lax.broadcasted_iota(jnp.int32, (1, width), 1)
    return (ch % S5_BLOCK_CH) < (S5_BLOCK_CH // 2)


def _tm_rows(ref, blk, r):
    return ref.at[blk, pl.ds(r, TOKEN_TILE, stride=SUBLANES), :]


_IN_SEGS = ((0, 512), (512, 2560), (2560, 4096), (4096, 7168))


def _in_proj_kernel(path, x_ref, mod_ref, g_ref, w_ref, u_ref, qkvg_ref, hy_ref, gl_ref):
    h = _rms_mod(x_ref[...], g_ref[...], mod_ref[0, 1:2, :], mod_ref[0, 0:1, :]).astype(BF16)
    u = _dot(h, w_ref[:, 0:512])
    r = path.row_in_step()
    m0 = _half_mask(S5_BLOCK_CH)
    for blk in range(S5_N_BLOCKS):
        ub = u[:, blk * S5_BLOCK_CH:(blk + 1) * S5_BLOCK_CH]
        if path.latent:
            _tm_rows(u_ref, blk, r)[...] = jnp.where(m0, ub, 0.0)
            _tm_rows(u_ref, blk, r + 1)[...] = jnp.where(m0, 0.0, ub)
        else:
            _tm_rows(u_ref, blk, r)[...] = ub
    for ref, (lo, hi) in zip((qkvg_ref, hy_ref, gl_ref), _IN_SEGS[1:]):
        ref[...] = _dot(h, w_ref[:, lo:hi]).astype(ref.dtype)


def _in_proj(path, layer, x, mod, norm_g, w_in):
    rows = x.shape[0]
    return pl.pallas_call(
        functools.partial(_in_proj_kernel, path),
        grid=(path.n_tiles,),
        in_specs=[
            pl.BlockSpec((TOKEN_TILE, D_MODEL), path.tok_index),
            path.mod_spec(layer),
            _layer_spec(norm_g.shape, layer),
            _layer_spec(w_in.shape, layer),
        ],
        out_specs=[
            path.tm_spec(),
            pl.BlockSpec((TOKEN_TILE, 2048), path.tok_index),
            pl.BlockSpec((TOKEN_TILE, 1536), path.tok_index),
            pl.BlockSpec((TOKEN_TILE, 3072), path.tok_index),
        ],
        out_shape=[
            path.tm_shape(),
            jax.ShapeDtypeStruct((rows, 2048), BF16),
            jax.ShapeDtypeStruct((rows, 1536), BF16),
            jax.ShapeDtypeStruct((rows, 3072), BF16),
        ],
        compiler_params=_cparams(1),
        name="in_proj_lat" if path.latent else "in_proj_ctx",
    )(x, mod, norm_g, w_in)


S5_CHUNK = 512


def _s5_kernel(seq, cols, has_h0, u_ref, b_ref, c_ref, lam_ref, *rest):
    if has_h0:
        h0_ref, y_ref, hfin_ref, st_f, st_b = rest
    else:
        y_ref, hfin_ref, st_f, st_b = rest
    n_par = hfin_ref.shape[0]
    seq_rows = seq * SUBLANES
    n_chunk = seq_rows // S5_CHUNK
    steps = S5_CHUNK // SUBLANES
    sts = (st_f, st_b)

    def chunk(p, ci):
        return pl.ds(pl.multiple_of(p * seq_rows + ci * S5_CHUNK, S5_CHUNK), S5_CHUNK)

    def expand(d, ci):
        for p in range(n_par):
            sts[d][chunk(p, ci), :] = _dot(u_ref[chunk(p, ci), :], b_ref[d, 0])

    def contract(d, ci):
        for p in range(n_par):
            y = _dot(sts[d][chunk(p, ci), :], c_ref[d, 0])
            if d == 0:
                y_ref[chunk(p, ci), :] = y
            else:
                y_ref[chunk(p, ci), :] += y

    def scan_chunk(d, ci, carry):
        carry = list(carry)
        lr, li = lam_ref[d, 0, 0], lam_ref[d, 0, 1]
        st = sts[d]
        for j in range(steps):
            i = ci * steps + j
            t = i if d == 0 else seq - 1 - i
            for p in range(n_par):
                hr, hi = carry[p]
                rows = pl.ds(pl.multiple_of(p * seq_rows + t * SUBLANES, SUBLANES), SUBLANES)
                nr = lr * hr - li * hi + st[rows, 0:cols]
                ni = lr * hi + li * hr + st[rows, cols:2 * cols]
                st[rows, 0:cols] = nr
                st[rows, cols:2 * cols] = ni
                carry[p] = (nr, ni)
        return tuple(carry)

    def init(d):
        if has_h0:
            return tuple((h0_ref[p, d, 0, 0], h0_ref[p, d, 0, 1]) for p in range(n_par))
        zero = jnp.zeros((SUBLANES, cols), F32)
        return ((zero, zero),) * n_par

    def finish(d, carry):
        for p in range(n_par):
            hfin_ref[p, d, 0, 0] = carry[p][0]
            hfin_ref[p, d, 0, 1] = carry[p][1]

    def phase_a(ci, carry):
        expand(0, ci)
        return carry

    def phase_b(ci, carry):
        expand(1, ci)
        return scan_chunk(0, ci, carry)

    def phase_c(ci, carry):
        contract(0, ci)
        return scan_chunk(1, ci, carry)

    def phase_d(ci, carry):
        contract(1, ci)
        return carry

    lax.fori_loop(0, n_chunk, phase_a, 0)
    finish(0, lax.fori_loop(0, n_chunk, phase_b, init(0)))
    finish(1, lax.fori_loop(0, n_chunk, phase_c, init(1)))
    lax.fori_loop(0, n_chunk, phase_d, 0)


def _s5(path, layer, u_tm, b_mat, c_mat, lam, h0=None):
    nb, cols = path.s5_blocks, path.s5_cols
    n_par = 2 if nb % 2 == 0 else 1
    rows = n_par * path.s5_rows
    st_shape = (nb, 2, S5_N_BLOCKS, 2, SUBLANES, cols)
    st_spec = pl.BlockSpec((n_par, 2, 1, 2, SUBLANES, cols), lambda b, g: (b, 0, g, 0, 0, 0))
    tm_spec = pl.BlockSpec((None, rows, S5_BLOCK_CH), lambda b, g: (g, b, 0))
    in_specs = [
        tm_spec,
        pl.BlockSpec((None, 2, 1, S5_BLOCK_CH, 2 * cols), lambda b, g: (layer, 0, g, 0, 0)),
        pl.BlockSpec((None, 2, 1, 2 * cols, S5_BLOCK_CH), lambda b, g: (layer, 0, g, 0, 0)),
        pl.BlockSpec((None, 2, 1, 2, SUBLANES, cols), lambda b, g: (layer, 0, g, 0, 0, 0)),
    ]
    args = [u_tm, b_mat, c_mat, lam]
    if h0 is not None:
        in_specs.append(pl.BlockSpec((None, 1, 2, 1, 2, SUBLANES, cols), lambda b, g: (layer, b, 0, g, 0, 0, 0)))
        args.append(h0)
    return pl.pallas_call(
        functools.partial(_s5_kernel, path.seq, cols, h0 is not None),
        grid=(nb // n_par, S5_N_BLOCKS),
        in_specs=in_specs,
        out_specs=[tm_spec, st_spec],
        out_shape=[path.tm_shape(), jax.ShapeDtypeStruct(st_shape, F32)],
        scratch_shapes=[pltpu.VMEM((rows, 2 * cols), F32), pltpu.VMEM((rows, 2 * cols), F32)],
        compiler_params=_cparams(2),
        name="s5_lat" if path.latent else "s5_ctx",
    )(*args)


def _s5_params(lam_re, lam_im, log_dt, b_re, b_im, c_re, c_im):
    lam = lax.complex(lam_re.astype(F32), lam_im.astype(F32))
    dt = jnp.exp(log_dt.astype(F32))[..., None]
    lam_bar = jnp.exp(lam * dt)
    b = lax.complex(b_re.astype(F32), b_im.astype(F32))
    b_bar = ((lam_bar - 1.0) / lam)[..., None] * b
    c = lax.complex(c_re.astype(F32), c_im.astype(F32))
    g8, nblk = S5_BLOCK_GROUPS, S5_N_BLOCKS
    cols = g8 * S5_STATE
    eye = np.eye(g8, dtype=np.float32)
    bb = b_bar.reshape(DEPTH, 2, nblk, g8, S5_STATE, S5_GROUP_CH)
    cc = c.reshape(DEPTH, 2, nblk, g8, S5_GROUP_CH, S5_STATE)

    def expand_b(part):
        return jnp.einsum('lrbgpc,gh->lrbgchp', part, eye).reshape(DEPTH, 2, nblk, S5_BLOCK_CH, cols)

    def expand_c(part):
        return jnp.einsum('lrbgcp,gh->lrbhpgc', part, eye).reshape(DEPTH, 2, nblk, cols, S5_BLOCK_CH)

    b_r, b_i = expand_b(jnp.real(bb)), expand_b(jnp.imag(bb))
    c_r, c_i = expand_c(jnp.real(cc)), -expand_c(jnp.imag(cc))
    half = cols // 2
    ctx = (jnp.concatenate([b_r, b_i], axis=-1).astype(BF16),
           jnp.concatenate([c_r, c_i], axis=-2).astype(BF16))
    lat = (jnp.concatenate([b_r[..., :half] + b_r[..., half:], b_i[..., :half] + b_i[..., half:]], axis=-1).astype(BF16),
           jnp.concatenate([c_r[..., :half, :] + c_r[..., half:, :], c_i[..., :half, :] + c_i[..., half:, :]],
                           axis=-2).astype(BF16))
    lb = jnp.stack([jnp.real(lam_bar), jnp.imag(lam_bar)], axis=2)
    lam_ctx = jnp.broadcast_to(lb.reshape(DEPTH, 2, 2, nblk, 1, cols), (DEPTH, 2, 2, nblk, SUBLANES, cols))
    lam_lat = jnp.broadcast_to(lb.reshape(DEPTH, 2, 2, nblk, 1, 2, half), (DEPTH, 2, 2, nblk, SUBLANES // 2, 2, half))
    lam_ctx = jnp.transpose(lam_ctx, (0, 1, 3, 2, 4, 5))
    lam_lat = jnp.transpose(lam_lat.reshape(DEPTH, 2, 2, nblk, SUBLANES, half), (0, 1, 3, 2, 4, 5))
    return ctx + (lam_ctx,), lat + (lam_lat,)


def _ret_kernel(seq, latent, n_q, layer, has_prev, lg_ref, q_ref, k_ref, v_ref, g_ref, *rest):
    if latent:
        cq_ref, sq_ref, ck_ref, sk_ref, s0_ref, o_ref, dec_ref = rest
    else:
        o_ref, st_ref, dec_ref = rest[1:] if has_prev else rest
    tq = TOKEN_TILE
    q_idx = pl.program_id(0) % n_q
    q0 = q_idx * tq
    scale = RET_DK ** -0.5
    trow = (q0 + lax.broadcasted_iota(jnp.int32, (tq, RET_DK), 0)).astype(F32)
    srow = lax.broadcasted_iota(jnp.int32, (seq, RET_DK), 0).astype(F32)

    @pl.when(pl.program_id(0) < n_q)
    def _():
        tpos = q0 + lax.broadcasted_iota(jnp.int32, (tq, seq), 0)
        spos = lax.broadcasted_iota(jnp.int32, (tq, seq), 1)
        diff = (tpos - spos).astype(F32)
        for h in range(RET_HEADS):
            lgf = lg_ref[layer, 0, h]
            lgb = lg_ref[layer, 1, h]
            dec_ref[q_idx, h] = jnp.exp(jnp.where(diff >= 0, lgf * diff, -lgb * diff)) * scale

    for h in range(RET_HEADS):
        sl = slice(h * RET_DK, (h + 1) * RET_DK)
        lgf = lg_ref[layer, 0, h]
        lgb = lg_ref[layer, 1, h]
        q = q_ref[:, sl].astype(F32)
        k = k_ref[:, sl].astype(F32)
        v = v_ref[:, sl]
        if latent:
            qr = q * cq_ref[...] + pltpu.roll(q, RET_DK // 2, axis=1) * sq_ref[...]
            kr = k * ck_ref[...] + pltpu.roll(k, RET_DK // 2, axis=1) * sk_ref[...]
        else:
            qr, kr = q, k
        scores = lax.dot_general(qr.astype(BF16), kr.astype(BF16), (((1,), (1,)), ((), ())),
                                 preferred_element_type=F32)
        out = _dot(scores * dec_ref[q_idx, h], v)
        if latent:
            out = out + _dot(q, s0_ref[0, 0, h]) * jnp.exp(lgf * (trow + 1.0))
            out = out + _dot(q, s0_ref[0, 1, h]) * jnp.exp(lgb * (seq - 1.0 - trow))
        else:
            kf = (k * (jnp.exp(lgf * (seq - 1.0 - srow)) * scale)).astype(BF16)
            kb = (k * (jnp.exp(lgb * srow) * scale)).astype(BF16)
            tn = (((0,), (0,)), ((), ()))
            st_ref[0, 0, h] = lax.dot_general(kf, v, tn, preferred_element_type=F32)
            st_ref[0, 1, h] = lax.dot_general(kb, v, tn, preferred_element_type=F32)
        xc = out - jnp.mean(out, axis=-1, keepdims=True)
        nrm = xc * lax.rsqrt(jnp.mean(xc * xc, axis=-1, keepdims=True) + GN_EPS)
        g = g_ref[:, sl].astype(F32)
        o_ref[:, sl] = (nrm * (g * jax.nn.sigmoid(g))).astype(o_ref.dtype)


def _rope_tables(seq):
    n_rows = seq // GRID_W
    rows = jnp.repeat(jnp.arange(n_rows, dtype=F32), GRID_W)
    cols = jnp.tile(jnp.arange(GRID_W, dtype=F32), n_rows)
    half = RET_DK // 2
    n_freq = half // 2
    inv = ROPE_BASE ** (-jnp.arange(n_freq, dtype=F32) / n_freq)
    ang = jnp.concatenate([rows[:, None] * inv, cols[:, None] * inv], axis=-1)
    cos = jnp.cos(ang)
    sin = jnp.sin(ang)
    return jnp.concatenate([cos, cos], axis=-1), jnp.concatenate([-sin, sin], axis=-1)


def _retention(path, layer, qkvg, log_g, rope=None, s0=None, states=None):
    seq, n_q = path.seq, path.tiles_per_seq
    rows = qkvg.shape[0]
    aliases = {}
    in_specs = [
        pl.BlockSpec(memory_space=pltpu.SMEM),
        pl.BlockSpec((TOKEN_TILE, RET_WIDTH), lambda i: (i, 0)),
        pl.BlockSpec((seq, RET_WIDTH), lambda i: (i // n_q, 1)),
        pl.BlockSpec((seq, RET_WIDTH), lambda i: (i // n_q, 2)),
        pl.BlockSpec((TOKEN_TILE, RET_WIDTH), lambda i: (i, 3)),
    ]
    args = [log_g, qkvg, qkvg, qkvg, qkvg]
    out_specs = [pl.BlockSpec((TOKEN_TILE, RET_WIDTH), lambda i: (i, 0))]
    out_shape = [jax.ShapeDtypeStruct((rows, RET_WIDTH), BF16)]
    if path.latent:
        cos, sin = rope
        in_specs += [
            pl.BlockSpec((TOKEN_TILE, RET_DK), lambda i: (i % n_q, 0)),
            pl.BlockSpec((TOKEN_TILE, RET_DK), lambda i: (i % n_q, 0)),
            _const_spec((seq, RET_DK)),
            _const_spec((seq, RET_DK)),
            pl.BlockSpec((1, None, 2, RET_HEADS, RET_DK, RET_DK), lambda i: (i // n_q, layer, 0, 0, 0, 0)),
        ]
        args += [cos, sin, cos, sin, s0]
    else:
        out_specs.append(pl.BlockSpec((1, None, 2, RET_HEADS, RET_DK, RET_DK), lambda i: (i, layer, 0, 0, 0, 0)))
        out_shape.append(jax.ShapeDtypeStruct(states.shape, F32))
        in_specs.append(pl.BlockSpec(memory_space=pl.ANY))
        args.append(states)
        aliases = {len(args) - 1: 1}
    return pl.pallas_call(
        functools.partial(_ret_kernel, seq, path.latent, n_q, layer, bool(aliases)),
        grid=(path.n_tiles,),
        in_specs=in_specs,
        out_specs=out_specs,
        out_shape=out_shape,
        scratch_shapes=[pltpu.VMEM((n_q, RET_HEADS, TOKEN_TILE, seq), F32)],
        input_output_aliases=aliases,
        compiler_params=_cparams(1),
        name="retention_lat" if path.latent else "retention_ctx",
    )(*args)


FILTER_COLS = 512
DFT_SPLIT = 32


def _dft_tables(seq):
    n = 2 * seq
    theta = 2.0 * math.pi / n

    def cos_sin(cols, stride):
        k = lax.broadcasted_iota(jnp.int32, (seq, cols), 0)
        j = lax.broadcasted_iota(jnp.int32, (seq, cols), 1)
        ang = ((k * (j * stride)) % n).astype(F32) * theta
        return jnp.cos(ang), jnp.sin(ang)

    (c1, s1), (c2, s2) = cos_sin(seq // DFT_SPLIT, DFT_SPLIT), cos_sin(DFT_SPLIT, 1)
    cos = (c1[:, :, None] * c2[:, None, :] - s1[:, :, None] * s2[:, None, :]).reshape(seq, seq)
    sin = (s1[:, :, None] * c2[:, None, :] + c1[:, :, None] * s2[:, None, :]).reshape(seq, seq)
    k = lax.broadcasted_iota(jnp.int32, (seq, seq), 0)
    t = lax.broadcasted_iota(jnp.int32, (seq, seq), 1)
    alt = jnp.where(t % 2 == 0, 1.0, -1.0)
    fwd = jnp.concatenate([cos, jnp.where(k == 0, alt, -sin)], axis=0)
    wk = jnp.where(k == 0, 1.0, 2.0) / n
    inv_c = (wk * cos).T
    inv_s = jnp.where(k == 0, alt / n, -(2.0 / n) * sin).T
    inv = jnp.concatenate([inv_c, inv_s], axis=1)
    return fwd.astype(BF16), inv.astype(BF16)


def _filter_kernel(seq, z_ref, w1_ref, b1_ref, w2_ref, b2_ref, fr_ref, w3f_ref, w3b_ref, rate_ref,
                   f_ref, p_ref, q_ref, nyq_ref):
    cb = p_ref.shape[-1]
    hid = jnp.sin(fr_ref[0, 0:1, :] * (_dot3(z_ref[...], w1_ref[0]) + b1_ref[0]))
    hid = jnp.sin(fr_ref[0, 1:2, :] * (_dot3(hid, w2_ref[0]) + b2_ref[0]))
    row = lax.broadcasted_iota(jnp.int32, (seq, cb), 0)
    win = jnp.exp(-(row.astype(F32) / seq) * rate_ref[...])
    fwd = _dot3(hid, w3f_ref[0]) * win
    bwd = jnp.where(row == 0, 0.0, _dot3(hid, w3b_ref[0]) * win)
    scale = lax.rsqrt(jnp.sum(fwd * fwd + bwd * bwd, axis=0, keepdims=True) + EPS)
    even = (fwd + bwd) * scale
    odd = (fwd - bwd) * scale
    p_ref[0] = _dot(f_ref[0:seq, :], even)
    q_ref[0] = jnp.where(row == 0, 0.0, _dot(f_ref[seq:2 * seq, :], odd))
    nyq_ref[0] = jnp.sum(jnp.where(row % 2 == 0, even, -even), axis=0, keepdims=True)


def _hyena_filters(seq, fwd, hy_w1, hy_b1, hy_w2, hy_b2, hy_freq, hy_w3):
    t = jnp.arange(seq, dtype=F32)
    bands = jnp.linspace(1e-4, HY_BANDS - 1, HY_BANDS, dtype=F32)
    ang = (2.0 * math.pi / seq) * t[:, None] * bands[None, :]
    z = jnp.concatenate([(t / seq)[:, None], jnp.cos(ang), -jnp.sin(ang),
                         jnp.zeros((seq, LANES - HY_EMB), F32)], axis=-1)
    w1 = jnp.pad(hy_w1.astype(F32), ((0, 0), (0, LANES - HY_EMB), (0, 0)))
    rate = jnp.linspace(HY_DECAY_MIN, HY_DECAY_MAX, HY_WIDTH, dtype=F32)
    rate = jnp.tile(rate, HY_ORDER).reshape(1, HY_ORDER * HY_WIDTH)
    width = HY_ORDER * HY_WIDTH
    n_cb = width // FILTER_COLS
    lay = lambda l, j: (l, 0, 0)
    out_spec = pl.BlockSpec((1, seq, FILTER_COLS), lambda l, j: (l, 0, j))
    return pl.pallas_call(
        functools.partial(_filter_kernel, seq),
        grid=(DEPTH, n_cb),
        in_specs=[
            pl.BlockSpec((seq, LANES), lambda l, j: (0, 0)),
            pl.BlockSpec((1, LANES, HY_HIDDEN), lay),
            pl.BlockSpec((1, 1, HY_HIDDEN), lay),
            pl.BlockSpec((1, HY_HIDDEN, HY_HIDDEN), lay),
            pl.BlockSpec((1, 1, HY_HIDDEN), lay),
            pl.BlockSpec((1, 2, HY_HIDDEN), lay),
            pl.BlockSpec((1, HY_HIDDEN, FILTER_COLS), lambda l, j: (l, 0, j)),
            pl.BlockSpec((1, HY_HIDDEN, FILTER_COLS), lambda l, j: (l, 0, n_cb + j)),
            pl.BlockSpec((1, FILTER_COLS), lambda l, j: (0, j)),
            _const_spec(fwd.shape),
        ],
        out_specs=[out_spec, out_spec, pl.BlockSpec((1, 1, FILTER_COLS), lambda l, j: (l, 0, j))],
        out_shape=[jax.ShapeDtypeStruct((DEPTH, seq, width), F32), jax.ShapeDtypeStruct((DEPTH, seq, width), F32),
                   jax.ShapeDtypeStruct((DEPTH, 1, width), F32)],
        compiler_params=_cparams(2),
        name="hyena_filter_%d" % seq,
    )(z, w1, hy_b1.astype(F32).reshape(DEPTH, 1, HY_HIDDEN), hy_w2.astype(F32),
      hy_b2.astype(F32).reshape(DEPTH, 1, HY_HIDDEN), hy_freq.astype(F32), hy_w3.astype(F32), hy_w3.astype(F32),
      rate, fwd)


def _hyena_kernel(seq, x1_ref, x2_ref, v_ref, w1_ref, w2_ref, wv_ref, b1_ref, b2_ref, bv_ref,
                  f_ref, g_ref, p1_ref, p2_ref, q1_ref, q2_ref, n1_ref, n2_ref, bias_ref, o_ref):
    cn = o_ref.shape[1]
    row = lax.broadcasted_iota(jnp.int32, (seq, cn), 0)

    def short_conv(x_ref, w_ref, b_ref):
        x = x_ref[...].astype(F32)
        prev = jnp.where(row == 0, 0.0, pltpu.roll(x, 1, axis=0))
        nxt = jnp.where(row == seq - 1, 0.0, pltpu.roll(x, seq - 1, axis=0))
        return prev * w_ref[0:1, :] + x * w_ref[1:2, :] + nxt * w_ref[2:3, :] + b_ref[...]

    out = short_conv(v_ref, wv_ref, bv_ref)
    stages = ((x1_ref, w1_ref, b1_ref, p1_ref, q1_ref, n1_ref), (x2_ref, w2_ref, b2_ref, p2_ref, q2_ref, n2_ref))
    for o, (x_ref, w_ref, b_ref, p_ref, q_ref, n_ref) in enumerate(stages):
        z = _dot(f_ref[...], out)
        zr, zi = z[:seq], z[seq:]
        p, q = p_ref[0], q_ref[0]
        yr = zr * p - zi * q
        yi = jnp.where(row == 0, zi * n_ref[0], zr * q + zi * p)
        y = jnp.concatenate([yr.astype(BF16), yi.astype(BF16)], axis=0)
        conv = _dot(g_ref[...], y)
        out = short_conv(x_ref, w_ref, b_ref) * (conv + bias_ref[0, o:o + 1, :] * out)
    o_ref[...] = out.astype(o_ref.dtype)


def _hyena(path, layer, hy, conv_w, conv_b, fwd_hi, inv, filt, bias):
    seq = path.seq
    cn = 512 if seq <= 256 else 256
    nj = HY_WIDTH // cn
    p, q, nyq = filt
    rows = hy.shape[0]

    def col(part):
        return lambda j, b: (b, part * nj + j)

    def wcol(part):
        return lambda j, b: (layer, 0, part * nj + j)

    def fcol(o):
        return lambda j, b: (layer, 0, o * nj + j)

    spec_f = [pl.BlockSpec((1, seq, cn), fcol(o)) for o in range(HY_ORDER)]
    spec_n = [pl.BlockSpec((1, 1, cn), fcol(o)) for o in range(HY_ORDER)]
    return pl.pallas_call(
        functools.partial(_hyena_kernel, seq),
        grid=(nj, path.batch),
        in_specs=[pl.BlockSpec((seq, cn), col(part)) for part in range(3)]
        + [pl.BlockSpec((None, 3, cn), wcol(part)) for part in range(3)]
        + [pl.BlockSpec((None, 1, cn), wcol(part)) for part in range(3)]
        + [_const_spec(fwd_hi.shape), _const_spec(inv.shape)] + spec_f + spec_f + spec_n
        + [pl.BlockSpec((1, HY_ORDER, cn), lambda j, b: (layer, 0, j))],
        out_specs=pl.BlockSpec((seq, cn), lambda j, b: (b, j)),
        out_shape=jax.ShapeDtypeStruct((rows, HY_WIDTH), BF16),
        compiler_params=_cparams(2),
        name="hyena_lat" if path.latent else "hyena_ctx",
    )(hy, hy, hy, conv_w, conv_w, conv_w, conv_b, conv_b, conv_b, fwd_hi, inv, p, p, q, q, nyq, nyq, bias)


def _merge_kernel(path, x_ref, mod_ref, y_ref, u_ref, d_ref, ret_ref, hy_ref, gl_ref,
                  wglu_ref, wret_ref, why_ref, wout_ref, o_ref):
    r = path.row_in_step()
    m0 = _half_mask(S5_BLOCK_CH)
    ys = []
    for blk in range(S5_N_BLOCKS):
        y, u = _tm_rows(y_ref, blk, r)[...], _tm_rows(u_ref, blk, r)[...]
        if path.latent:
            y = jnp.where(m0, y, _tm_rows(y_ref, blk, r + 1)[...])
            u = u + _tm_rows(u_ref, blk, r + 1)[...]
        ys.append(y + d_ref[:, blk * S5_BLOCK_CH:(blk + 1) * S5_BLOCK_CH] * u)
    y_s5 = jnp.concatenate(ys, axis=1)
    ab = _dot(jax.nn.gelu(y_s5), wglu_ref[...])
    br_s5 = ab[:, :D_MODEL] * _sigmoid(ab[:, D_MODEL:])
    br_ret = _dot(ret_ref[...], wret_ref[...])
    br_hy = _dot(hy_ref[...], why_ref[...])
    gates = _sigmoid(gl_ref[...].astype(F32))
    merged = (gates[:, 0:D_MODEL] * br_s5 + gates[:, D_MODEL:2 * D_MODEL] * br_ret
              + gates[:, 2 * D_MODEL:] * br_hy)
    mix = _dot(merged, wout_ref[...])
    o_ref[...] = x_ref[...] + mod_ref[0, 2:3, :] * mix


def _merge(path, layer, x, mod, y_tm, u_tm, d, ret, hyo, gl, w_glu, w_ret, w_hy, w_out):
    rows = x.shape[0]
    return pl.pallas_call(
        functools.partial(_merge_kernel, path),
        grid=(path.n_tiles,),
        in_specs=[
            pl.BlockSpec((TOKEN_TILE, D_MODEL), path.tok_index),
            path.mod_spec(layer),
            path.tm_spec(), path.tm_spec(),
            _layer_spec(d.shape, layer),
            pl.BlockSpec((TOKEN_TILE, RET_WIDTH), path.tok_index),
            pl.BlockSpec((TOKEN_TILE, HY_WIDTH), path.tok_index),
            pl.BlockSpec((TOKEN_TILE, 3 * D_MODEL), path.tok_index),
            _layer_spec(w_glu.shape, layer), _layer_spec(w_ret.shape, layer),
            _layer_spec(w_hy.shape, layer), _layer_spec(w_out.shape, layer),
        ],
        out_specs=pl.BlockSpec((TOKEN_TILE, D_MODEL), path.tok_index),
        out_shape=jax.ShapeDtypeStruct((rows, D_MODEL), F32),
        compiler_params=_cparams(1),
        name="merge_lat" if path.latent else "merge_ctx",
    )(x, mod, y_tm, u_tm, d, ret, hyo, gl, w_glu, w_ret, w_hy, w_out)


def _ffn_kernel(final, x_ref, mod_ref, g_ref, win_ref, wout_ref, gf_ref, o_ref):
    x = x_ref[...]
    h = _rms_mod(x, g_ref[...], mod_ref[0, 4:5, :], mod_ref[0, 3:4, :]).astype(BF16)
    a = _dot(h, win_ref[:, :D_FF])
    b = _dot(h, win_ref[:, D_FF:])
    act = (a * jax.nn.sigmoid(a)) * b
    x = x + mod_ref[0, 5:6, :] * _dot(act, wout_ref[...])
    if final:
        x = (x * lax.rsqrt(jnp.mean(x * x, axis=-1, keepdims=True) + EPS)) * gf_ref[...]
    o_ref[...] = x


def _ffn(path, layer, x, mod, norm_g, w_in, w_out, norm_f, final):
    rows = x.shape[0]
    tok = lambda i: (i, 0)
    return pl.pallas_call(
        functools.partial(_ffn_kernel, final),
        grid=(path.n_tiles,),
        in_specs=[
            pl.BlockSpec((TOKEN_TILE, D_MODEL), tok),
            path.mod_spec(layer, natural=True),
            _layer_spec(norm_g.shape, layer),
            _layer_spec(w_in.shape, layer), _layer_spec(w_out.shape, layer),
            _const_spec((1, D_MODEL)),
        ],
        out_specs=pl.BlockSpec((TOKEN_TILE, D_MODEL), tok),
        out_shape=jax.ShapeDtypeStruct((rows, D_MODEL), F32),
        compiler_params=_cparams(1),
        name="ffn_lat" if path.latent else "ffn_ctx",
    )(x, mod, norm_g, w_in, w_out, norm_f)


def kernel(x_prompt, x_sample, state_s5, state_ret, c, c_ctx, w_mod, b_mod, norm1, norm2, w_in, s5_lam_re, s5_lam_im, s5_log_dt, s5_b_re, s5_b_im, s5_c_re, s5_c_im, s5_d, w_s5_glu, ret_decay, w_ret_o, hy_conv_w, hy_conv_b, hy_w1, hy_b1, hy_w2, hy_b2, hy_freq, hy_w3, hy_bias, w_hy_o, w_out, w_ffn_in, w_ffn_out, norm_f):
    batch, seq, _ = x_prompt.shape
    dec_batch, dec_seq, _ = x_sample.shape
    ctx = _Path(batch, seq, latent=False)
    lat = _Path(dec_batch, dec_seq, latent=True)

    c_rows = jnp.concatenate([c_ctx[None, :], c, jnp.zeros((SUBLANES - 1 - dec_batch, D_MODEL), F32)], axis=0)
    mod_all = _modulation(c_rows.astype(F32), w_mod, b_mod.astype(F32))
    mod_all = mod_all.reshape(DEPTH, SUBLANES, 6, D_MODEL)

    s5_par = dict(zip((ctx, lat), _s5_params(s5_lam_re, s5_lam_im, s5_log_dt, s5_b_re, s5_b_im, s5_c_re, s5_c_im)))
    tables, filters = {}, {}
    for path in (ctx, lat):
        tables[path] = _dft_tables(path.seq)
        filters[path] = _hyena_filters(path.seq, tables[path][0], hy_w1, hy_b1, hy_w2, hy_b2, hy_freq, hy_w3)
    rope = _rope_tables(dec_seq)
    log_g = jnp.log1p(-jnp.exp(ret_decay.astype(F32)))
    norm_f2 = norm_f.astype(F32).reshape(1, D_MODEL)
    conv_w = hy_conv_w.astype(F32)
    conv_b = hy_conv_b.astype(F32).reshape(DEPTH, 1, 3 * HY_WIDTH)
    hy_bias = hy_bias.astype(F32)
    d_skip = s5_d.astype(F32).reshape(DEPTH, 1, S5_WIDTH)
    n1 = norm1.astype(F32).reshape(DEPTH, 1, D_MODEL)
    n2 = norm2.astype(F32).reshape(DEPTH, 1, D_MODEL)

    st = state_s5.astype(F32).reshape(dec_batch, DEPTH, 2, S5_N_BLOCKS, 2, lat.s5_cols, 2)
    h0_lat = jnp.transpose(st, (1, 2, 3, 6, 0, 4, 5)).reshape(DEPTH, 1, 2, S5_N_BLOCKS, 2, SUBLANES, lat.s5_cols)
    s0_ret = state_ret.astype(F32)

    w_in_b, w_glu_b, w_ret_b = w_in, w_s5_glu, w_ret_o
    w_hy_b, w_out_b = w_hy_o, w_out
    w_fin_b, w_fout_b = w_ffn_in, w_ffn_out

    xs = {ctx: x_prompt.astype(F32).reshape(batch * seq, D_MODEL),
          lat: x_sample.astype(F32).reshape(dec_batch * dec_seq, D_MODEL)}
    s5_states = []
    ret_state = jnp.zeros((batch, DEPTH, 2, RET_HEADS, RET_DK, RET_DK), F32)
    for l in range(DEPTH):
        for path in (ctx, lat):
            x = xs[path]
            u_tm, qkvg, hy, gl = _in_proj(path, l, x, mod_all, n1, w_in_b)
            b_mat, c_mat, lam_t = s5_par[path]
            y_tm, h_fin = _s5(path, l, u_tm, b_mat, c_mat, lam_t, h0_lat if path.latent else None)
            if path.latent:
                ret, = _retention(path, l, qkvg, log_g, rope, s0_ret)
            else:
                ret, ret_state = _retention(path, l, qkvg, log_g, states=ret_state)
                s5_states.append(h_fin)
            fwd, inv = tables[path]
            hyo = _hyena(path, l, hy, conv_w, conv_b, fwd, inv, filters[path], hy_bias)
            x = _merge(path, l, x, mod_all, y_tm, u_tm, d_skip, ret, hyo, gl, w_glu_b, w_ret_b, w_hy_b, w_out_b)
            xs[path] = _ffn(path, l, x, mod_all, n2, w_fin_b, w_fout_b, norm_f2, final=(l == DEPTH - 1))

    y_prompt = xs[ctx].reshape(batch, seq, D_MODEL).astype(x_prompt.dtype)
    y_sample = xs[lat].reshape(dec_batch, dec_seq, D_MODEL).astype(x_sample.dtype)
    hf = jnp.stack(s5_states, axis=0).reshape(DEPTH, ctx.s5_blocks, 2, S5_N_BLOCKS, 2, SUBLANES, S5_BLOCK_GROUPS, S5_STATE)
    hf = jnp.transpose(hf, (1, 5, 0, 2, 3, 6, 7, 4)).reshape(batch, DEPTH, 2, S5_GROUPS, S5_STATE, 2)
    new_state_s5 = hf.astype(x_prompt.dtype)
    new_state_ret = ret_state.astype(x_prompt.dtype)
    return (y_prompt, y_sample, new_state_s5, new_state_ret)
```

```python
import functools
import math

import jax
import jax.numpy as jnp
import numpy as np
from jax import lax
from jax.experimental import pallas as pl
from jax.experimental.pallas import tpu as pltpu

F32 = jnp.float32
BF16 = jnp.bfloat16

D_MODEL = 1024
DEPTH = 2
GRID_W = 64
EPS = 1e-6
GN_EPS = 1e-5
S5_WIDTH = 512
S5_GROUP_CH = 16
S5_GROUPS = 32
S5_STATE = 64
RET_WIDTH = 512
RET_HEADS = 4
RET_DK = 128
ROPE_BASE = 10000.0
HY_WIDTH = 512
HY_ORDER = 2
HY_BANDS = 16
HY_EMB = 1 + 2 * HY_BANDS
HY_HIDDEN = 64
HY_DECAY_MIN = -math.log(1e-2) / 1.5
HY_DECAY_MAX = -math.log(1e-2) / 0.3
D_FF = 2816

LANES = 128
SUBLANES = 8
TOKEN_TILE = 256
S5_BLOCK_CH = LANES
S5_BLOCK_GROUPS = S5_BLOCK_CH // S5_GROUP_CH
S5_N_BLOCKS = S5_WIDTH // S5_BLOCK_CH
VMEM_LIMIT = 56 * 1024 * 1024


def _cparams(n_axes):
    return pltpu.CompilerParams(dimension_semantics=("arbitrary",) * n_axes, vmem_limit_bytes=VMEM_LIMIT)


def _const_spec(shape):
    nd = len(shape)
    return pl.BlockSpec(shape, lambda *_: (0,) * nd, pipeline_mode=pl.Buffered(1))


def _layer_spec(shape, layer):
    nd = len(shape) - 1
    return pl.BlockSpec((None,) + tuple(shape[1:]), lambda *_: (layer,) + (0,) * nd, pipeline_mode=pl.Buffered(1))


def _dot(a, b):
    return jnp.dot(a.astype(BF16), b.astype(BF16), preferred_element_type=F32)


def _split(a):
    hi = a.astype(BF16)
    return hi, (a - hi.astype(F32)).astype(BF16)


def _dot3(a, b):
    a_hi, a_lo = _split(a)
    b_hi, b_lo = _split(b)
    dot = functools.partial(jnp.dot, preferred_element_type=F32)
    return dot(a_hi, b_hi) + (dot(a_hi, b_lo) + dot(a_lo, b_hi))


def _sigmoid(x):
    return 0.5 * jnp.tanh(0.5 * x) + 0.5


def _rms_mod(x, g, scale, shift):
    y = x * lax.rsqrt(jnp.mean(x * x, axis=-1, keepdims=True) + EPS)
    return (y * g) * (1.0 + scale) + shift


def _mod_kernel(c_ref, w_ref, b_ref, o_ref):
    c = c_ref[...]
    cond = c * jax.nn.sigmoid(c)
    o_ref[0] = _dot(cond, w_ref[0]) + b_ref[0]


def _modulation(c_rows, w_mod, b_mod):
    n_chunk = 6
    return pl.pallas_call(
        _mod_kernel,
        grid=(DEPTH, n_chunk),
        in_specs=[
            pl.BlockSpec((SUBLANES, D_MODEL), lambda l, j: (0, 0)),
            pl.BlockSpec((1, D_MODEL, D_MODEL), lambda l, j: (l, 0, j)),
            pl.BlockSpec((1, 1, D_MODEL), lambda l, j: (l, 0, j)),
        ],
        out_specs=pl.BlockSpec((1, SUBLANES, D_MODEL), lambda l, j: (l, 0, j)),
        out_shape=jax.ShapeDtypeStruct((DEPTH, SUBLANES, 6 * D_MODEL), F32),
        compiler_params=_cparams(2),
        name="modulation",
    )(c_rows, w_mod, b_mod.reshape(DEPTH, 1, 6 * D_MODEL))


class _Path:
    def __init__(self, batch, seq, latent):
        self.batch = batch
        self.seq = seq
        self.latent = latent
        self.tiles_per_seq = seq // TOKEN_TILE
        self.n_tiles = batch * self.tiles_per_seq
        if latent:
            assert batch * 2 == SUBLANES
            self.s5_blocks = 1
            self.s5_cols = S5_BLOCK_GROUPS * S5_STATE // 2
            self.tiles_per_tm = batch
        else:
            assert batch % SUBLANES == 0 and seq == TOKEN_TILE
            self.s5_blocks = batch // SUBLANES
            self.s5_cols = S5_BLOCK_GROUPS * S5_STATE
            self.tiles_per_tm = SUBLANES
        self.s5_rows = seq * SUBLANES
        self.tm_rows = TOKEN_TILE * SUBLANES

    def tok(self, i):
        if self.latent:
            return (i % self.batch) * self.tiles_per_seq + i // self.batch
        return i

    def tok_index(self, i):
        return (self.tok(i), 0)

    def mod_spec(self, layer, natural=False):
        def index(i):
            seq_id = i // self.tiles_per_seq if natural else i % self.batch
            return (layer, 1 + seq_id if self.latent else 0, 0, 0)
        return pl.BlockSpec((None, 1, 6, D_MODEL), index)

    def tm_index(self, i):
        return (0, i // self.tiles_per_tm, 0)

    def tm_spec(self):
        return pl.BlockSpec((S5_N_BLOCKS, self.tm_rows, S5_BLOCK_CH), self.tm_index)

    def tm_shape(self):
        return jax.ShapeDtypeStruct((S5_N_BLOCKS, self.s5_blocks * self.s5_rows, S5_BLOCK_CH), F32)

    def row_in_step(self):
        r = pl.program_id(0) % self.tiles_per_tm
        return 2 * r if self.latent else r


def _half_mask(width):
    ch = lax.broadcasted_iota(jnp.int32, (1, width), 1)
    return (ch % S5_BLOCK_CH) < (S5_BLOCK_CH // 2)


def _tm_rows(ref, blk, r):
    return ref.at[blk, pl.ds(r, TOKEN_TILE, stride=SUBLANES), :]


_IN_SEGS = ((0, 512), (512, 2560), (2560, 4096), (4096, 7168))


def _in_proj_kernel(path, x_ref, mod_ref, g_ref, w_ref, u_ref, qkvg_ref, hy_ref, gl_ref):
    h = _rms_mod(x_ref[...], g_ref[...], mod_ref[0, 1:2, :], mod_ref[0, 0:1, :]).astype(BF16)
    u = _dot(h, w_ref[:, 0:512])
    r = path.row_in_step()
    m0 = _half_mask(S5_BLOCK_CH)
    for blk in range(S5_N_BLOCKS):
        ub = u[:, blk * S5_BLOCK_CH:(blk + 1) * S5_BLOCK_CH]
        if path.latent:
            _tm_rows(u_ref, blk, r)[...] = jnp.where(m0, ub, 0.0)
            _tm_rows(u_ref, blk, r + 1)[...] = jnp.where(m0, 0.0, ub)
        else:
            _tm_rows(u_ref, blk, r)[...] = ub
    for ref, (lo, hi) in zip((qkvg_ref, hy_ref, gl_ref), _IN_SEGS[1:]):
        ref[...] = _dot(h, w_ref[:, lo:hi]).astype(ref.dtype)


def _in_proj(path, layer, x, mod, norm_g, w_in):
    rows = x.shape[0]
    return pl.pallas_call(
        functools.partial(_in_proj_kernel, path),
        grid=(path.n_tiles,),
        in_specs=[
            pl.BlockSpec((TOKEN_TILE, D_MODEL), path.tok_index),
            path.mod_spec(layer),
            _layer_spec(norm_g.shape, layer),
            _layer_spec(w_in.shape, layer),
        ],
        out_specs=[
            path.tm_spec(),
            pl.BlockSpec((TOKEN_TILE, 2048), path.tok_index),
            pl.BlockSpec((TOKEN_TILE, 1536), path.tok_index),
            pl.BlockSpec((TOKEN_TILE, 3072), path.tok_index),
        ],
        out_shape=[
            path.tm_shape(),
            jax.ShapeDtypeStruct((rows, 2048), BF16),
            jax.ShapeDtypeStruct((rows, 1536), BF16),
            jax.ShapeDtypeStruct((rows, 3072), BF16),
        ],
        compiler_params=_cparams(1),
        name="in_proj_lat" if path.latent else "in_proj_ctx",
    )(x, mod, norm_g, w_in)


S5_CHUNK = 512


def _s5_stream_kernel(seq, cols, has_h0, dir_groups, u_ref, b_ref, c_ref, lam_ref, *rest):
    if has_h0:
        h0_ref, y_ref, hfin_ref = rest[:3]
    else:
        y_ref, hfin_ref = rest[:2]
    bufs = rest[-8:]
    bu = (bufs[0:2], bufs[2:4])
    hs = (bufs[4:6], bufs[6:8])
    n = seq * SUBLANES // S5_CHUNK
    steps = S5_CHUNK // SUBLANES
    assert n >= 4 and n % 2 == 0

    def chunk(d, k):
        c = k if d == 0 else n - 1 - k
        return pl.ds(pl.multiple_of(c * S5_CHUNK, S5_CHUNK), S5_CHUNK)

    def run(dirs):
        def expand(k, slot):
            for d in dirs:
                bu[d][slot][...] = _dot(u_ref[chunk(d, k), :], b_ref[d, 0])

        def contract(k, slot):
            for d in dirs:
                y_ref[chunk(d, k), :] += _dot(hs[d][slot][...], c_ref[d, 0])

        def scan(slot, carry):
            carry = list(carry)
            for j in range(steps):
                for i, d in enumerate(dirs):
                    hr, hi = carry[i]
                    lr, li = lam_ref[d, 0, 0], lam_ref[d, 0, 1]
                    rows = pl.ds((j if d == 0 else steps - 1 - j) * SUBLANES, SUBLANES)
                    nr = lr * hr - li * hi + bu[d][slot][rows, 0:cols]
                    ni = lr * hi + li * hr + bu[d][slot][rows, cols:2 * cols]
                    hs[d][slot][rows, 0:cols] = nr
                    hs[d][slot][rows, cols:2 * cols] = ni
                    carry[i] = (nr, ni)
            return tuple(carry)

        def step(k, slot, carry, first=False, last=False):
            if not last:
                expand(k + 1, 1 - slot)
            carry = scan(slot, carry)
            if not first:
                contract(k - 1, 1 - slot)
            return carry

        def pair(i, carry):
            k = 1 + 2 * i
            return step(k + 1, 0, step(k, 1, carry))

        if has_h0:
            carry = tuple((h0_ref[0, d, 0, 0], h0_ref[0, d, 0, 1]) for d in dirs)
        else:
            zero = jnp.zeros((SUBLANES, cols), F32)
            carry = ((zero, zero),) * len(dirs)
        expand(0, 0)
        carry = step(0, 0, carry, first=True)
        carry = lax.fori_loop(0, (n - 2) // 2, pair, carry)
        carry = step(n - 1, 1, carry, last=True)
        contract(n - 1, 1)
        for i, d in enumerate(dirs):
            hfin_ref[0, d, 0, 0] = carry[i][0]
            hfin_ref[0, d, 0, 1] = carry[i][1]

    y_ref[...] = jnp.zeros(y_ref.shape, F32)
    for dirs in dir_groups:
        run(dirs)


def _s5(path, layer, u_tm, b_mat, c_mat, lam, h0=None):
    nb, cols = path.s5_blocks, path.s5_cols
    n_par = 1
    rows = n_par * path.s5_rows
    dir_groups = ((0, 1),)
    st_shape = (nb, 2, S5_N_BLOCKS, 2, SUBLANES, cols)
    st_spec = pl.BlockSpec((n_par, 2, 1, 2, SUBLANES, cols), lambda b, g: (b, 0, g, 0, 0, 0))
    tm_spec = pl.BlockSpec((None, rows, S5_BLOCK_CH), lambda b, g: (g, b, 0))
    in_specs = [
        tm_spec,
        pl.BlockSpec((None, 2, 1, S5_BLOCK_CH, 2 * cols), lambda b, g: (layer, 0, g, 0, 0)),
        pl.BlockSpec((None, 2, 1, 2 * cols, S5_BLOCK_CH), lambda b, g: (layer, 0, g, 0, 0)),
        pl.BlockSpec((None, 2, 1, 2, SUBLANES, cols), lambda b, g: (layer, 0, g, 0, 0, 0)),
    ]
    args = [u_tm, b_mat, c_mat, lam]
    if h0 is not None:
        in_specs.append(pl.BlockSpec((None, 1, 2, 1, 2, SUBLANES, cols), lambda b, g: (layer, b, 0, g, 0, 0, 0)))
        args.append(h0)
    return pl.pallas_call(
        functools.partial(_s5_stream_kernel, path.seq, cols, h0 is not None, dir_groups),
        grid=(nb // n_par, S5_N_BLOCKS),
        in_specs=in_specs,
        out_specs=[tm_spec, st_spec],
        out_shape=[path.tm_shape(), jax.ShapeDtypeStruct(st_shape, F32)],
        scratch_shapes=[pltpu.VMEM((S5_CHUNK, 2 * cols), F32)] * 8,
        compiler_params=_cparams(2),
        name="s5_lat" if path.latent else "s5_ctx",
    )(*args)


def _s5_params(lam_re, lam_im, log_dt, b_re, b_im, c_re, c_im):
    lam = lax.complex(lam_re.astype(F32), lam_im.astype(F32))
    dt = jnp.exp(log_dt.astype(F32))[..., None]
    lam_bar = jnp.exp(lam * dt)
    b = lax.complex(b_re.astype(F32), b_im.astype(F32))
    b_bar = ((lam_bar - 1.0) / lam)[..., None] * b
    c = lax.complex(c_re.astype(F32), c_im.astype(F32))
    g8, nblk = S5_BLOCK_GROUPS, S5_N_BLOCKS
    cols = g8 * S5_STATE
    eye = np.eye(g8, dtype=np.float32)
    bb = b_bar.reshape(DEPTH, 2, nblk, g8, S5_STATE, S5_GROUP_CH)
    cc = c.reshape(DEPTH, 2, nblk, g8, S5_GROUP_CH, S5_STATE)

    def expand_b(part):
        return jnp.einsum('lrbgpc,gh->lrbgchp', part, eye).reshape(DEPTH, 2, nblk, S5_BLOCK_CH, cols)

    def expand_c(part):
        return jnp.einsum('lrbgcp,gh->lrbhpgc', part, eye).reshape(DEPTH, 2, nblk, cols, S5_BLOCK_CH)

    b_r, b_i = expand_b(jnp.real(bb)), expand_b(jnp.imag(bb))
    c_r, c_i = expand_c(jnp.real(cc)), -expand_c(jnp.imag(cc))
    half = cols // 2
    ctx = (jnp.concatenate([b_r, b_i], axis=-1).astype(BF16),
           jnp.concatenate([c_r, c_i], axis=-2).astype(BF16))
    lat = (jnp.concatenate([b_r[..., :half] + b_r[..., half:], b_i[..., :half] + b_i[..., half:]], axis=-1).astype(BF16),
           jnp.concatenate([c_r[..., :half, :] + c_r[..., half:, :], c_i[..., :half, :] + c_i[..., half:, :]],
                           axis=-2).astype(BF16))
    lb = jnp.stack([jnp.real(lam_bar), jnp.imag(lam_bar)], axis=2)
    lam_ctx = jnp.broadcast_to(lb.reshape(DEPTH, 2, 2, nblk, 1, cols), (DEPTH, 2, 2, nblk, SUBLANES, cols))
    lam_lat = jnp.broadcast_to(lb.reshape(DEPTH, 2, 2, nblk, 1, 2, half), (DEPTH, 2, 2, nblk, SUBLANES // 2, 2, half))
    lam_ctx = jnp.transpose(lam_ctx, (0, 1, 3, 2, 4, 5))
    lam_lat = jnp.transpose(lam_lat.reshape(DEPTH, 2, 2, nblk, SUBLANES, half), (0, 1, 3, 2, 4, 5))
    return ctx + (lam_ctx,), lat + (lam_lat,)


def _ret_kernel(seq, latent, n_q, layer, has_prev, lg_ref, q_ref, k_ref, v_ref, g_ref, *rest):
    if latent:
        cq_ref, sq_ref, ck_ref, sk_ref, s0_ref, o_ref, dec_ref = rest
    else:
        o_ref, st_ref, dec_ref = rest[1:] if has_prev else rest
    tq = TOKEN_TILE
    q_idx = pl.program_id(0) % n_q
    q0 = q_idx * tq
    scale = RET_DK ** -0.5
    trow = (q0 + lax.broadcasted_iota(jnp.int32, (tq, RET_DK), 0)).astype(F32)
    srow = lax.broadcasted_iota(jnp.int32, (seq, RET_DK), 0).astype(F32)

    @pl.when(pl.program_id(0) < n_q)
    def _():
        tpos = q0 + lax.broadcasted_iota(jnp.int32, (tq, seq), 0)
        spos = lax.broadcasted_iota(jnp.int32, (tq, seq), 1)
        diff = (tpos - spos).astype(F32)
        for h in range(RET_HEADS):
            lgf = lg_ref[layer, 0, h]
            lgb = lg_ref[layer, 1, h]
            dec_ref[q_idx, h] = jnp.exp(jnp.where(diff >= 0, lgf * diff, -lgb * diff)) * scale

    for h in range(RET_HEADS):
        sl = slice(h * RET_DK, (h + 1) * RET_DK)
        lgf = lg_ref[layer, 0, h]
        lgb = lg_ref[layer, 1, h]
        q = q_ref[:, sl].astype(F32)
        k = k_ref[:, sl].astype(F32)
        v = v_ref[:, sl]
        if latent:
            qr = q * cq_ref[...] + pltpu.roll(q, RET_DK // 2, axis=1) * sq_ref[...]
            kr = k * ck_ref[...] + pltpu.roll(k, RET_DK // 2, axis=1) * sk_ref[...]
        else:
            qr, kr = q, k
        scores = lax.dot_general(qr.astype(BF16), kr.astype(BF16), (((1,), (1,)), ((), ())),
                                 preferred_element_type=F32)
        out = _dot(scores * dec_ref[q_idx, h], v)
        if latent:
            out = out + _dot(q, s0_ref[0, 0, h]) * jnp.exp(lgf * (trow + 1.0))
            out = out + _dot(q, s0_ref[0, 1, h]) * jnp.exp(lgb * (seq - 1.0 - trow))
        else:
            kf = (k * (jnp.exp(lgf * (seq - 1.0 - srow)) * scale)).astype(BF16)
            kb = (k * (jnp.exp(lgb * srow) * scale)).astype(BF16)
            tn = (((0,), (0,)), ((), ()))
            st_ref[0, 0, h] = lax.dot_general(kf, v, tn, preferred_element_type=F32)
            st_ref[0, 1, h] = lax.dot_general(kb, v, tn, preferred_element_type=F32)
        xc = out - jnp.mean(out, axis=-1, keepdims=True)
        nrm = xc * lax.rsqrt(jnp.mean(xc * xc, axis=-1, keepdims=True) + GN_EPS)
        g = g_ref[:, sl].astype(F32)
        o_ref[:, sl] = (nrm * (g * jax.nn.sigmoid(g))).astype(o_ref.dtype)


def _rope_tables(seq):
    n_rows = seq // GRID_W
    rows = jnp.repeat(jnp.arange(n_rows, dtype=F32), GRID_W)
    cols = jnp.tile(jnp.arange(GRID_W, dtype=F32), n_rows)
    half = RET_DK // 2
    n_freq = half // 2
    inv = ROPE_BASE ** (-jnp.arange(n_freq, dtype=F32) / n_freq)
    ang = jnp.concatenate([rows[:, None] * inv, cols[:, None] * inv], axis=-1)
    cos = jnp.cos(ang)
    sin = jnp.sin(ang)
    return jnp.concatenate([cos, cos], axis=-1), jnp.concatenate([-sin, sin], axis=-1)


def _retention(path, layer, qkvg, log_g, rope=None, s0=None, states=None):
    seq, n_q = path.seq, path.tiles_per_seq
    rows = qkvg.shape[0]
    aliases = {}
    in_specs = [
        pl.BlockSpec(memory_space=pltpu.SMEM),
        pl.BlockSpec((TOKEN_TILE, RET_WIDTH), lambda i: (i, 0)),
        pl.BlockSpec((seq, RET_WIDTH), lambda i: (i // n_q, 1)),
        pl.BlockSpec((seq, RET_WIDTH), lambda i: (i // n_q, 2)),
        pl.BlockSpec((TOKEN_TILE, RET_WIDTH), lambda i: (i, 3)),
    ]
    args = [log_g, qkvg, qkvg, qkvg, qkvg]
    out_specs = [pl.BlockSpec((TOKEN_TILE, RET_WIDTH), lambda i: (i, 0))]
    out_shape = [jax.ShapeDtypeStruct((rows, RET_WIDTH), BF16)]
    if path.latent:
        cos, sin = rope
        in_specs += [
            pl.BlockSpec((TOKEN_TILE, RET_DK), lambda i: (i % n_q, 0)),
            pl.BlockSpec((TOKEN_TILE, RET_DK), lambda i: (i % n_q, 0)),
            _const_spec((seq, RET_DK)),
            _const_spec((seq, RET_DK)),
            pl.BlockSpec((1, None, 2, RET_HEADS, RET_DK, RET_DK), lambda i: (i // n_q, layer, 0, 0, 0, 0)),
        ]
        args += [cos, sin, cos, sin, s0]
    else:
        out_specs.append(pl.BlockSpec((1, None, 2, RET_HEADS, RET_DK, RET_DK), lambda i: (i, layer, 0, 0, 0, 0)))
        out_shape.append(jax.ShapeDtypeStruct(states.shape, F32))
        in_specs.append(pl.BlockSpec(memory_space=pl.ANY))
        args.append(states)
        aliases = {len(args) - 1: 1}
    return pl.pallas_call(
        functools.partial(_ret_kernel, seq, path.latent, n_q, layer, bool(aliases)),
        grid=(path.n_tiles,),
        in_specs=in_specs,
        out_specs=out_specs,
        out_shape=out_shape,
        scratch_shapes=[pltpu.VMEM((n_q, RET_HEADS, TOKEN_TILE, seq), F32)],
        input_output_aliases=aliases,
        compiler_params=_cparams(1),
        name="retention_lat" if path.latent else "retention_ctx",
    )(*args)


FILTER_COLS = 512
DFT_SPLIT = 32


def _dft_tables(seq):
    n = 2 * seq
    theta = 2.0 * math.pi / n

    def cos_sin(cols, stride):
        k = lax.broadcasted_iota(jnp.int32, (seq, cols), 0)
        j = lax.broadcasted_iota(jnp.int32, (seq, cols), 1)
        ang = ((k * (j * stride)) % n).astype(F32) * theta
        return jnp.cos(ang), jnp.sin(ang)

    (c1, s1), (c2, s2) = cos_sin(seq // DFT_SPLIT, DFT_SPLIT), cos_sin(DFT_SPLIT, 1)
    cos = (c1[:, :, None] * c2[:, None, :] - s1[:, :, None] * s2[:, None, :]).reshape(seq, seq)
    sin = (s1[:, :, None] * c2[:, None, :] + c1[:, :, None] * s2[:, None, :]).reshape(seq, seq)
    k = lax.broadcasted_iota(jnp.int32, (seq, seq), 0)
    t = lax.broadcasted_iota(jnp.int32, (seq, seq), 1)
    alt = jnp.where(t % 2 == 0, 1.0, -1.0)
    fwd = jnp.concatenate([cos, jnp.where(k == 0, alt, -sin)], axis=0)
    wk = jnp.where(k == 0, 1.0, 2.0) / n
    inv_c = (wk * cos).T
    inv_s = jnp.where(k == 0, alt / n, -(2.0 / n) * sin).T
    inv = jnp.concatenate([inv_c, inv_s], axis=1)
    return fwd.astype(BF16), inv.astype(BF16)


def _filter_kernel(seq, z_ref, w1_ref, b1_ref, w2_ref, b2_ref, fr_ref, w3f_ref, w3b_ref, rate_ref,
                   f_ref, p_ref, q_ref, nyq_ref):
    cb = p_ref.shape[-1]
    hid = jnp.sin(fr_ref[0, 0:1, :] * (_dot3(z_ref[...], w1_ref[0]) + b1_ref[0]))
    hid = jnp.sin(fr_ref[0, 1:2, :] * (_dot3(hid, w2_ref[0]) + b2_ref[0]))
    row = lax.broadcasted_iota(jnp.int32, (seq, cb), 0)
    win = jnp.exp(-(row.astype(F32) / seq) * rate_ref[...])
    fwd = _dot3(hid, w3f_ref[0]) * win
    bwd = jnp.where(row == 0, 0.0, _dot3(hid, w3b_ref[0]) * win)
    scale = lax.rsqrt(jnp.sum(fwd * fwd + bwd * bwd, axis=0, keepdims=True) + EPS)
    even = (fwd + bwd) * scale
    odd = (fwd - bwd) * scale
    p_ref[0] = _dot(f_ref[0:seq, :], even)
    q_ref[0] = jnp.where(row == 0, 0.0, _dot(f_ref[seq:2 * seq, :], odd))
    nyq_ref[0] = jnp.sum(jnp.where(row % 2 == 0, even, -even), axis=0, keepdims=True)


def _hyena_filters(seq, fwd, hy_w1, hy_b1, hy_w2, hy_b2, hy_freq, hy_w3):
    t = jnp.arange(seq, dtype=F32)
    bands = jnp.linspace(1e-4, HY_BANDS - 1, HY_BANDS, dtype=F32)
    ang = (2.0 * math.pi / seq) * t[:, None] * bands[None, :]
    z = jnp.concatenate([(t / seq)[:, None], jnp.cos(ang), -jnp.sin(ang),
                         jnp.zeros((seq, LANES - HY_EMB), F32)], axis=-1)
    w1 = jnp.pad(hy_w1.astype(F32), ((0, 0), (0, LANES - HY_EMB), (0, 0)))
    rate = jnp.linspace(HY_DECAY_MIN, HY_DECAY_MAX, HY_WIDTH, dtype=F32)
    rate = jnp.tile(rate, HY_ORDER).reshape(1, HY_ORDER * HY_WIDTH)
    width = HY_ORDER * HY_WIDTH
    n_cb = width // FILTER_COLS
    lay = lambda l, j: (l, 0, 0)
    out_spec = pl.BlockSpec((1, seq, FILTER_COLS), lambda l, j: (l, 0, j))
    return pl.pallas_call(
        functools.partial(_filter_kernel, seq),
        grid=(DEPTH, n_cb),
        in_specs=[
            pl.BlockSpec((seq, LANES), lambda l, j: (0, 0)),
            pl.BlockSpec((1, LANES, HY_HIDDEN), lay),
            pl.BlockSpec((1, 1, HY_HIDDEN), lay),
            pl.BlockSpec((1, HY_HIDDEN, HY_HIDDEN), lay),
            pl.BlockSpec((1, 1, HY_HIDDEN), lay),
            pl.BlockSpec((1, 2, HY_HIDDEN), lay),
            pl.BlockSpec((1, HY_HIDDEN, FILTER_COLS), lambda l, j: (l, 0, j)),
            pl.BlockSpec((1, HY_HIDDEN, FILTER_COLS), lambda l, j: (l, 0, n_cb + j)),
            pl.BlockSpec((1, FILTER_COLS), lambda l, j: (0, j)),
            _const_spec(fwd.shape),
        ],
        out_specs=[out_spec, out_spec, pl.BlockSpec((1, 1, FILTER_COLS), lambda l, j: (l, 0, j))],
        out_shape=[jax.ShapeDtypeStruct((DEPTH, seq, width), F32), jax.ShapeDtypeStruct((DEPTH, seq, width), F32),
                   jax.ShapeDtypeStruct((DEPTH, 1, width), F32)],
        compiler_params=_cparams(2),
        name="hyena_filter_%d" % seq,
    )(z, w1, hy_b1.astype(F32).reshape(DEPTH, 1, HY_HIDDEN), hy_w2.astype(F32),
      hy_b2.astype(F32).reshape(DEPTH, 1, HY_HIDDEN), hy_freq.astype(F32), hy_w3.astype(F32), hy_w3.astype(F32),
      rate, fwd)


def _hyena_kernel(seq, slab, x1_ref, x2_ref, v_ref, w1_ref, w2_ref, wv_ref, b1_ref, b2_ref, bv_ref,
                  f_ref, g_ref, p1_ref, p2_ref, q1_ref, q2_ref, n1_ref, n2_ref, bias_ref, o_ref):
    row = lax.broadcasted_iota(jnp.int32, (seq, slab), 0)
    stages = ((x1_ref, w1_ref, b1_ref, p1_ref, q1_ref, n1_ref), (x2_ref, w2_ref, b2_ref, p2_ref, q2_ref, n2_ref))
    for c0 in range(0, o_ref.shape[1], slab):
        cs = slice(c0, c0 + slab)

        def short_conv(x_ref, w_ref, b_ref):
            x = x_ref[:, cs].astype(F32)
            prev = jnp.where(row == 0, 0.0, pltpu.roll(x, 1, axis=0))
            nxt = jnp.where(row == seq - 1, 0.0, pltpu.roll(x, seq - 1, axis=0))
            return prev * w_ref[0:1, cs] + x * w_ref[1:2, cs] + nxt * w_ref[2:3, cs] + b_ref[:, cs]

        out = short_conv(v_ref, wv_ref, bv_ref)
        for o, (x_ref, w_ref, b_ref, p_ref, q_ref, n_ref) in enumerate(stages):
            z = _dot(f_ref[...], out)
            zr, zi = z[:seq], z[seq:]
            p, q = p_ref[0, :, cs], q_ref[0, :, cs]
            yr = zr * p - zi * q
            yi = jnp.where(row == 0, zi * n_ref[0, :, cs], zr * q + zi * p)
            y = jnp.concatenate([yr.astype(BF16), yi.astype(BF16)], axis=0)
            conv = _dot(g_ref[...], y)
            out = short_conv(x_ref, w_ref, b_ref) * (conv + bias_ref[0, o:o + 1, cs] * out)
        o_ref[:, cs] = out.astype(o_ref.dtype)


def _hyena(path, layer, hy, conv_w, conv_b, fwd_hi, inv, filt, bias):
    seq = path.seq
    cn = 512
    nj = HY_WIDTH // cn
    p, q, nyq = filt
    rows = hy.shape[0]

    def col(part):
        return lambda j, b: (b, part * nj + j)

    def wcol(part):
        return lambda j, b: (layer, 0, part * nj + j)

    def fcol(o):
        return lambda j, b: (layer, 0, o * nj + j)

    spec_f = [pl.BlockSpec((1, seq, cn), fcol(o)) for o in range(HY_ORDER)]
    spec_n = [pl.BlockSpec((1, 1, cn), fcol(o)) for o in range(HY_ORDER)]
    return pl.pallas_call(
        functools.partial(_hyena_kernel, seq, cn if seq <= 256 else cn // 2),
        grid=(nj, path.batch),
        in_specs=[pl.BlockSpec((seq, cn), col(part)) for part in range(3)]
        + [pl.BlockSpec((None, 3, cn), wcol(part)) for part in range(3)]
        + [pl.BlockSpec((None, 1, cn), wcol(part)) for part in range(3)]
        + [_const_spec(fwd_hi.shape), _const_spec(inv.shape)] + spec_f + spec_f + spec_n
        + [pl.BlockSpec((1, HY_ORDER, cn), lambda j, b: (layer, 0, j))],
        out_specs=pl.BlockSpec((seq, cn), lambda j, b: (b, j)),
        out_shape=jax.ShapeDtypeStruct((rows, HY_WIDTH), BF16),
        compiler_params=_cparams(2),
        name="hyena_lat" if path.latent else "hyena_ctx",
    )(hy, hy, hy, conv_w, conv_w, conv_w, conv_b, conv_b, conv_b, fwd_hi, inv, p, p, q, q, nyq, nyq, bias)


def _merge_kernel(path, x_ref, mod_ref, y_ref, u_ref, d_ref, ret_ref, hy_ref, gl_ref,
                  wglu_ref, wret_ref, why_ref, wout_ref, o_ref):
    r = path.row_in_step()
    m0 = _half_mask(S5_BLOCK_CH)
    ys = []
    for blk in range(S5_N_BLOCKS):
        y, u = _tm_rows(y_ref, blk, r)[...], _tm_rows(u_ref, blk, r)[...]
        if path.latent:
            y = jnp.where(m0, y, _tm_rows(y_ref, blk, r + 1)[...])
            u = u + _tm_rows(u_ref, blk, r + 1)[...]
        ys.append(y + d_ref[:, blk * S5_BLOCK_CH:(blk + 1) * S5_BLOCK_CH] * u)
    y_s5 = jnp.concatenate(ys, axis=1)
    ab = _dot(jax.nn.gelu(y_s5), wglu_ref[...])
    br_s5 = ab[:, :D_MODEL] * _sigmoid(ab[:, D_MODEL:])
    br_ret = _dot(ret_ref[...], wret_ref[...])
    br_hy = _dot(hy_ref[...], why_ref[...])
    gates = _sigmoid(gl_ref[...].astype(F32))
    merged = (gates[:, 0:D_MODEL] * br_s5 + gates[:, D_MODEL:2 * D_MODEL] * br_ret
              + gates[:, 2 * D_MODEL:] * br_hy)
    mix = _dot(merged, wout_ref[...])
    o_ref[...] = x_ref[...] + mod_ref[0, 2:3, :] * mix


def _merge(path, layer, x, mod, y_tm, u_tm, d, ret, hyo, gl, w_glu, w_ret, w_hy, w_out):
    rows = x.shape[0]
    return pl.pallas_call(
        functools.partial(_merge_kernel, path),
        grid=(path.n_tiles,),
        in_specs=[
            pl.BlockSpec((TOKEN_TILE, D_MODEL), path.tok_index),
            path.mod_spec(layer),
            path.tm_spec(), path.tm_spec(),
            _layer_spec(d.shape, layer),
            pl.BlockSpec((TOKEN_TILE, RET_WIDTH), path.tok_index),
            pl.BlockSpec((TOKEN_TILE, HY_WIDTH), path.tok_index),
            pl.BlockSpec((TOKEN_TILE, 3 * D_MODEL), path.tok_index),
            _layer_spec(w_glu.shape, layer), _layer_spec(w_ret.shape, layer),
            _layer_spec(w_hy.shape, layer), _layer_spec(w_out.shape, layer),
        ],
        out_specs=pl.BlockSpec((TOKEN_TILE, D_MODEL), path.tok_index),
        out_shape=jax.ShapeDtypeStruct((rows, D_MODEL), F32),
        compiler_params=_cparams(1),
        name="merge_lat" if path.latent else "merge_ctx",
    )(x, mod, y_tm, u_tm, d, ret, hyo, gl, w_glu, w_ret, w_hy, w_out)


def _ffn_kernel(final, x_ref, mod_ref, g_ref, win_ref, wout_ref, gf_ref, o_ref):
    x = x_ref[...]
    h = _rms_mod(x, g_ref[...], mod_ref[0, 4:5, :], mod_ref[0, 3:4, :]).astype(BF16)
    a = _dot(h, win_ref[:, :D_FF])
    b = _dot(h, win_ref[:, D_FF:])
    act = (a * jax.nn.sigmoid(a)) * b
    x = x + mod_ref[0, 5:6, :] * _dot(act, wout_ref[...])
    if final:
        x = (x * lax.rsqrt(jnp.mean(x * x, axis=-1, keepdims=True) + EPS)) * gf_ref[...]
    o_ref[...] = x


def _ffn(path, layer, x, mod, norm_g, w_in, w_out, norm_f, final):
    rows = x.shape[0]
    tok = lambda i: (i, 0)
    return pl.pallas_call(
        functools.partial(_ffn_kernel, final),
        grid=(path.n_tiles,),
        in_specs=[
            pl.BlockSpec((TOKEN_TILE, D_MODEL), tok),
            path.mod_spec(layer, natural=True),
            _layer_spec(norm_g.shape, layer),
            _layer_spec(w_in.shape, layer), _layer_spec(w_out.shape, layer),
            _const_spec((1, D_MODEL)),
        ],
        out_specs=pl.BlockSpec((TOKEN_TILE, D_MODEL), tok),
        out_shape=jax.ShapeDtypeStruct((rows, D_MODEL), F32),
        compiler_params=_cparams(1),
        name="ffn_lat" if path.latent else "ffn_ctx",
    )(x, mod, norm_g, w_in, w_out, norm_f)


def kernel(x_prompt, x_sample, state_s5, state_ret, c, c_ctx, w_mod, b_mod, norm1, norm2, w_in, s5_lam_re, s5_lam_im, s5_log_dt, s5_b_re, s5_b_im, s5_c_re, s5_c_im, s5_d, w_s5_glu, ret_decay, w_ret_o, hy_conv_w, hy_conv_b, hy_w1, hy_b1, hy_w2, hy_b2, hy_freq, hy_w3, hy_bias, w_hy_o, w_out, w_ffn_in, w_ffn_out, norm_f):
    batch, seq, _ = x_prompt.shape
    dec_batch, dec_seq, _ = x_sample.shape
    ctx = _Path(batch, seq, latent=False)
    lat = _Path(dec_batch, dec_seq, latent=True)

    c_rows = jnp.concatenate([c_ctx[None, :], c, jnp.zeros((SUBLANES - 1 - dec_batch, D_MODEL), F32)], axis=0)
    mod_all = _modulation(c_rows.astype(F32), w_mod, b_mod.astype(F32))
    mod_all = mod_all.reshape(DEPTH, SUBLANES, 6, D_MODEL)

    s5_par = dict(zip((ctx, lat), _s5_params(s5_lam_re, s5_lam_im, s5_log_dt, s5_b_re, s5_b_im, s5_c_re, s5_c_im)))
    tables, filters = {}, {}
    for path in (ctx, lat):
        tables[path] = _dft_tables(path.seq)
        filters[path] = _hyena_filters(path.seq, tables[path][0], hy_w1, hy_b1, hy_w2, hy_b2, hy_freq, hy_w3)
    rope = _rope_tables(dec_seq)
    log_g = jnp.log1p(-jnp.exp(ret_decay.astype(F32)))
    norm_f2 = norm_f.astype(F32).reshape(1, D_MODEL)
    conv_w = hy_conv_w.astype(F32)
    conv_b = hy_conv_b.astype(F32).reshape(DEPTH, 1, 3 * HY_WIDTH)
    hy_bias = hy_bias.astype(F32)
    d_skip = s5_d.astype(F32).reshape(DEPTH, 1, S5_WIDTH)
    n1 = norm1.astype(F32).reshape(DEPTH, 1, D_MODEL)
    n2 = norm2.astype(F32).reshape(DEPTH, 1, D_MODEL)

    st = state_s5.astype(F32).reshape(dec_batch, DEPTH, 2, S5_N_BLOCKS, 2, lat.s5_cols, 2)
    h0_lat = jnp.transpose(st, (1, 2, 3, 6, 0, 4, 5)).reshape(DEPTH, 1, 2, S5_N_BLOCKS, 2, SUBLANES, lat.s5_cols)
    s0_ret = state_ret.astype(F32)

    w_in_b, w_glu_b, w_ret_b = w_in, w_s5_glu, w_ret_o
    w_hy_b, w_out_b = w_hy_o, w_out
    w_fin_b, w_fout_b = w_ffn_in, w_ffn_out

    xs = {ctx: x_prompt.astype(F32).reshape(batch * seq, D_MODEL),
          lat: x_sample.astype(F32).reshape(dec_batch * dec_seq, D_MODEL)}
    s5_states = []
    ret_state = jnp.zeros((batch, DEPTH, 2, RET_HEADS, RET_DK, RET_DK), F32)
    for l in range(DEPTH):
        for path in (ctx, lat):
            x = xs[path]
            u_tm, qkvg, hy, gl = _in_proj(path, l, x, mod_all, n1, w_in_b)
            b_mat, c_mat, lam_t = s5_par[path]
            y_tm, h_fin = _s5(path, l, u_tm, b_mat, c_mat, lam_t, h0_lat if path.latent else None)
            if path.latent:
                ret, = _retention(path, l, qkvg, log_g, rope, s0_ret)
            else:
                ret, ret_state = _retention(path, l, qkvg, log_g, states=ret_state)
                s5_states.append(h_fin)
            fwd, inv = tables[path]
            hyo = _hyena(path, l, hy, conv_w, conv_b, fwd, inv, filters[path], hy_bias)
            x = _merge(path, l, x, mod_all, y_tm, u_tm, d_skip, ret, hyo, gl, w_glu_b, w_ret_b, w_hy_b, w_out_b)
            xs[path] = _ffn(path, l, x, mod_all, n2, w_fin_b, w_fout_b, norm_f2, final=(l == DEPTH - 1))

    y_prompt = xs[ctx].reshape(batch, seq, D_MODEL).astype(x_prompt.dtype)
    y_sample = xs[lat].reshape(dec_batch, dec_seq, D_MODEL).astype(x_sample.dtype)
    hf = jnp.stack(s5_states, axis=0).reshape(DEPTH, ctx.s5_blocks, 2, S5_N_BLOCKS, 2, SUBLANES, S5_BLOCK_GROUPS, S5_STATE)
    hf = jnp.transpose(hf, (1, 5, 0, 2, 3, 6, 7, 4)).reshape(batch, DEPTH, 2, S5_GROUPS, S5_STATE, 2)
    new_state_s5 = hf.astype(x_prompt.dtype)
    new_state_ret = ret_state.astype(x_prompt.dtype)
    return (y_prompt, y_sample, new_state_s5, new_state_ret)
```
